```python
import jax, jax.numpy as jnp
from jax import lax
import numpy as np

D_MODEL = 2048
BATCH = 2
SEQ = 16384
DEPTH = 2

PLE_DIM = 256
N_HEADS = 8
HEAD_DIM = 128
ATTN_WIDTH = N_HEADS * HEAD_DIM
ROPE_DIM = HEAD_DIM // 4
ROPE_THETA = 500000.0
MOBA_BLOCK = 256
MOBA_TOPK = 3
Q_CHUNK = 64
CONV_CHANNELS = 512
CONV_KERNEL = 31
SGU_GROUPS = 4
SGU_GROUP_DIM = 128
SGU_WIDTH = SGU_GROUPS * SGU_GROUP_DIM
SGU_CHUNK = 128
D_FF = 5632
FFN_CONV_KERNEL = 3
N_BRANCHES = 3
IN_WIDTH = 3 * ATTN_WIDTH + 2 * CONV_CHANNELS + 2 * SGU_WIDTH + N_BRANCHES * D_MODEL
EPS = 1e-6

kernel_name = "hybrid_gated_conv_sgu_moba_block"


def rms_norm(x, g):
    xf = x.astype(jnp.float32)
    y = xf * lax.rsqrt(jnp.mean(xf * xf, axis=-1, keepdims=True) + EPS)
    return (y * g.astype(jnp.float32)).astype(x.dtype)


def layer_norm(x, g, b):
    xf = x.astype(jnp.float32)
    mu = jnp.mean(xf, axis=-1, keepdims=True)
    var = jnp.mean(jnp.square(xf - mu), axis=-1, keepdims=True)
    y = (xf - mu) * lax.rsqrt(var + EPS)
    return (y * g.astype(jnp.float32) + b.astype(jnp.float32)).astype(x.dtype)


def causal_dwconv(x, w, b):
    k = w.shape[0]
    y = lax.conv_general_dilated(
        x, w[:, None, :].astype(x.dtype), window_strides=(1,), padding=[(k - 1, 0)],
        dimension_numbers=("NWC", "WIO", "NWC"), feature_group_count=x.shape[-1])
    return y + b.astype(x.dtype)


def partial_rope(x, pos):
    half = ROPE_DIM // 2
    inv = jnp.float32(ROPE_THETA) ** (-jnp.arange(0, ROPE_DIM, 2, dtype=jnp.float32) / ROPE_DIM)
    ang = pos.astype(jnp.float32)[:, None] * inv[None, :]
    cos, sin = jnp.cos(ang), jnp.sin(ang)
    xr = x[..., :ROPE_DIM].astype(jnp.float32)
    x1, x2 = xr[..., :half], xr[..., half:]
    rot = jnp.concatenate([x1 * cos - x2 * sin, x2 * cos + x1 * sin], axis=-1).astype(x.dtype)
    return jnp.concatenate([rot, x[..., ROPE_DIM:]], axis=-1)


def moba_attention(q, k, v):
    bsz, nh, s, dh = q.shape
    nb = -(-s // MOBA_BLOCK)
    pad = nb * MOBA_BLOCK - s
    kb = jnp.pad(k, ((0, 0), (0, 0), (0, pad), (0, 0))).reshape(bsz, nh, nb, MOBA_BLOCK, dh)
    vb = jnp.pad(v, ((0, 0), (0, 0), (0, pad), (0, 0))).reshape(bsz, nh, nb, MOBA_BLOCK, dh)
    k_mean = jnp.mean(kb.astype(jnp.float32), axis=3).astype(q.dtype)
    topk = min(MOBA_TOPK, nb)
    gather = jax.vmap(jax.vmap(lambda blocks, idx: blocks[idx]))
    sel_len = topk * MOBA_BLOCK

    def chunk(c):
        q0 = c * Q_CHUNK
        qc = lax.dynamic_slice_in_dim(q, q0, Q_CHUNK, axis=2)
        own = q0 // MOBA_BLOCK
        k_own = lax.dynamic_index_in_dim(kb, own, axis=2, keepdims=False)
        v_own = lax.dynamic_index_in_dim(vb, own, axis=2, keepdims=False)
        gate = jnp.einsum("bhqd,bhnd->bhqn", qc, k_mean).astype(jnp.float32)
        gate = jnp.where(jnp.arange(nb) < own, gate, -jnp.inf)
        _, idx = lax.top_k(gate, topk)
        valid = jnp.arange(topk) < own
        k_sel = gather(kb, idx)
        v_sel = gather(vb, idx)
        s_sel = jnp.einsum("bhqd,bhqjkd->bhqjk", qc, k_sel).astype(jnp.float32)
        s_sel = jnp.where(valid[:, None], s_sel, -jnp.inf)
        q_pos = q0 + jnp.arange(Q_CHUNK)
        k_pos = own * MOBA_BLOCK + jnp.arange(MOBA_BLOCK)
        s_own = jnp.einsum("bhqd,bhkd->bhqk", qc, k_own).astype(jnp.float32)
        s_own = jnp.where(k_pos[None, :] <= q_pos[:, None], s_own, -jnp.inf)
        scores = jnp.concatenate([s_sel.reshape(bsz, nh, Q_CHUNK, sel_len), s_own], axis=-1)
        probs = jax.nn.softmax(scores, axis=-1).astype(v.dtype)
        p_sel = probs[..., :sel_len].reshape(bsz, nh, Q_CHUNK, topk, MOBA_BLOCK)
        p_own = probs[..., sel_len:]
        return (jnp.einsum("bhqjk,bhqjkd->bhqd", p_sel, v_sel)
                + jnp.einsum("bhqk,bhkd->bhqd", p_own, v_own))

    outs = lax.map(chunk, jnp.arange(s // Q_CHUNK))
    return outs.transpose(1, 0, 3, 2, 4).reshape(bsz, s, nh * dh)


def token_mixer(xn, w_in, conv_dw_w, conv_dw_b, conv_norm_g, conv_norm_b, conv_out,
                sgu_norm_g, sgu_norm_b, sgu_w, sgu_b, sgu_out, attn_out, w_o):
    bsz, s, _ = xn.shape
    proj = xn @ w_in
    cuts = np.cumsum([ATTN_WIDTH, ATTN_WIDTH, ATTN_WIDTH, CONV_CHANNELS, CONV_CHANNELS,
                      SGU_WIDTH, SGU_WIDTH, D_MODEL, D_MODEL]).tolist()
    q, k, v, ca, cg, su, sv, g_a, g_b, g_c = jnp.split(proj, cuts, axis=-1)

    a = ca * jax.nn.sigmoid(cg)
    a = causal_dwconv(a, conv_dw_w, conv_dw_b)
    a = jax.nn.silu(layer_norm(a, conv_norm_g, conv_norm_b))
    y_a = a @ conv_out

    su = jax.nn.gelu(su, approximate=True)
    sv = layer_norm(jax.nn.gelu(sv, approximate=True), sgu_norm_g, sgu_norm_b)
    svr = sv.reshape(bsz, s // SGU_CHUNK, SGU_CHUNK, SGU_GROUPS, SGU_GROUP_DIM)
    mask = jnp.tril(jnp.ones((SGU_CHUNK, SGU_CHUNK), dtype=sgu_w.dtype))
    mixed = jnp.einsum("gts,bnsgc->bntgc", sgu_w * mask, svr) + sgu_b.T[None, None, :, :, None]
    y_b = (su * mixed.reshape(bsz, s, SGU_WIDTH)) @ sgu_out

    pos = jnp.arange(s, dtype=jnp.int32)
    to_heads = lambda t: t.reshape(bsz, s, N_HEADS, HEAD_DIM).transpose(0, 2, 1, 3)
    qh = partial_rope(to_heads(q), pos) * jnp.asarray(HEAD_DIM ** -0.5, dtype=q.dtype)
    kh = partial_rope(to_heads(k), pos)
    y_c = moba_attention(qh, kh, to_heads(v)) @ attn_out

    merged = jax.nn.sigmoid(g_a) * y_a + jax.nn.sigmoid(g_b) * y_b + jax.nn.sigmoid(g_c) * y_c
    return merged @ w_o


def conv_ffn(hn, ffn_in, ffn_dw_w, ffn_dw_b, ffn_out):
    a, b = jnp.split(hn @ ffn_in, 2, axis=-1)
    a = causal_dwconv(a, ffn_dw_w, ffn_dw_b)
    return (jax.nn.gelu(a, approximate=True) * b) @ ffn_out


def setup_inputs(seed: int = 0) -> dict:
    key = jax.random.key(seed)
    ks = jax.random.split(key, 32)
    f32 = jnp.float32
    nrm = lambda k, shape, fan_in: jax.random.normal(k, shape, f32) * (fan_in ** -0.5)
    gain = lambda k, shape: 1.0 + 0.05 * jax.random.normal(k, shape, f32)
    small = lambda k, shape: 0.02 * jax.random.normal(k, shape, f32)
    L = DEPTH
    return {
        "x": jax.random.normal(ks[0], (BATCH, SEQ, D_MODEL), f32),
        "p": jax.random.normal(ks[1], (DEPTH, BATCH, SEQ, PLE_DIM), f32),
        "mix_norm_pre": gain(ks[2], (L, D_MODEL)),
        "mix_norm_post": gain(ks[3], (L, D_MODEL)),
        "w_in": nrm(ks[4], (L, D_MODEL, IN_WIDTH), D_MODEL),
        "conv_dw_w": nrm(ks[5], (L, CONV_KERNEL, CONV_CHANNELS), CONV_KERNEL),
        "conv_dw_b": small(ks[6], (L, CONV_CHANNELS)),
        "conv_norm_g": gain(ks[7], (L, CONV_CHANNELS)),
        "conv_norm_b": small(ks[8], (L, CONV_CHANNELS)),
        "conv_out": nrm(ks[9], (L, CONV_CHANNELS, D_MODEL), CONV_CHANNELS),
        "sgu_norm_g": gain(ks[10], (L, SGU_WIDTH)),
        "sgu_norm_b": small(ks[11], (L, SGU_WIDTH)),
        "sgu_w": nrm(ks[12], (L, SGU_GROUPS, SGU_CHUNK, SGU_CHUNK), SGU_CHUNK),
        "sgu_b": small(ks[13], (L, SGU_GROUPS, SGU_CHUNK)),
        "sgu_out": nrm(ks[14], (L, SGU_WIDTH, D_MODEL), SGU_WIDTH),
        "attn_out": nrm(ks[15], (L, ATTN_WIDTH, D_MODEL), ATTN_WIDTH),
        "w_o": nrm(ks[16], (L, D_MODEL, D_MODEL), D_MODEL),
        "ffn_norm_pre": gain(ks[17], (L, D_MODEL)),
        "ffn_norm_post": gain(ks[18], (L, D_MODEL)),
        "ffn_in": nrm(ks[19], (L, D_MODEL, 2 * D_FF), D_MODEL),
        "ffn_dw_w": nrm(ks[20], (L, FFN_CONV_KERNEL, D_FF), FFN_CONV_KERNEL),
        "ffn_dw_b": small(ks[21], (L, D_FF)),
        "ffn_out": nrm(ks[22], (L, D_FF, D_MODEL), D_FF),
        "ple_norm": gain(ks[23], (L, D_MODEL)),
        "ple_gate": nrm(ks[24], (L, D_MODEL, D_MODEL), D_MODEL),
        "ple_proj": nrm(ks[25], (L, PLE_DIM, D_MODEL), PLE_DIM),
    }


def reference(x, p, mix_norm_pre, mix_norm_post, w_in, conv_dw_w, conv_dw_b, conv_norm_g,
              conv_norm_b, conv_out, sgu_norm_g, sgu_norm_b, sgu_w, sgu_b, sgu_out, attn_out,
              w_o, ffn_norm_pre, ffn_norm_post, ffn_in, ffn_dw_w, ffn_dw_b, ffn_out,
              ple_norm, ple_gate, ple_proj):
    for i in range(DEPTH):
        xn = rms_norm(x, mix_norm_pre[i])
        h = token_mixer(xn, w_in[i], conv_dw_w[i], conv_dw_b[i], conv_norm_g[i], conv_norm_b[i],
                        conv_out[i], sgu_norm_g[i], sgu_norm_b[i], sgu_w[i], sgu_b[i], sgu_out[i],
                        attn_out[i], w_o[i])
        x = x + rms_norm(h, mix_norm_post[i])
        hn = rms_norm(x, ffn_norm_pre[i])
        f = conv_ffn(hn, ffn_in[i], ffn_dw_w[i], ffn_dw_b[i], ffn_out[i])
        x = x + rms_norm(f, ffn_norm_post[i])
        e = p[i] @ ple_proj[i]
        g = jax.nn.sigmoid(rms_norm(x, ple_norm[i]) @ ple_gate[i])
        x = x + g * e
    return x
```

```python
import functools

import jax
import jax.numpy as jnp
from jax import lax
from jax.experimental import pallas as pl
from jax.experimental.pallas import tpu as pltpu

F32 = jnp.float32
BF16 = jnp.bfloat16

D_MODEL = 2048
PLE_DIM = 256
N_HEADS = 8
HEAD_DIM = 128
ATTN_WIDTH = N_HEADS * HEAD_DIM
ROPE_DIM = HEAD_DIM // 4
ROPE_THETA = 500000.0
MOBA_BLOCK = 256
MOBA_TOPK = 3
CONV_CHANNELS = 512
CONV_KERNEL = 31
SGU_GROUPS = 4
SGU_GROUP_DIM = 128
SGU_WIDTH = SGU_GROUPS * SGU_GROUP_DIM
SGU_CHUNK = 128
D_FF = 5632
FFN_CONV_KERNEL = 3
EPS = 1e-6

VMEM_LIMIT_BYTES = 56 * 1024 * 1024
LANES = 128
BF16_SUBLANES = 16
MASK_VALUE = -1e30

CONV_HALO = 32
FFN_HALO = BF16_SUBLANES


def _cparams(*sem):
    return pltpu.CompilerParams(dimension_semantics=sem, vmem_limit_bytes=VMEM_LIMIT_BYTES)


def _rms(x, g):
    return x * lax.rsqrt(jnp.mean(x * x, axis=-1, keepdims=True) + EPS) * g


def _layer_norm(x, g, b):
    mu = jnp.mean(x, axis=-1, keepdims=True)
    xc = x - mu
    var = jnp.mean(xc * xc, axis=-1, keepdims=True)
    return xc * lax.rsqrt(var + EPS) * g + b


def _gelu_tanh(x):
    c = 0.7978845608028654
    return 0.5 * x * (1.0 + jnp.tanh(c * (x + 0.044715 * (x * x * x))))


def _sigmoid(x):
    return 1.0 / (1.0 + jnp.exp(-x))


def _dot(a, b):
    return jnp.dot(a, b, preferred_element_type=F32)


def _dot_nt(a, b):
    return lax.dot_general(a, b, (((1,), (1,)), ((), ())), preferred_element_type=F32)


def _prenorm_kernel(x_ref, g_ref, o_ref):
    o_ref[...] = _rms(x_ref[...], g_ref[...]).astype(BF16)


def _prenorm(x, g, tm=512):
    t, d = x.shape
    return pl.pallas_call(
        _prenorm_kernel,
        grid=(t // tm,),
        in_specs=[pl.BlockSpec((tm, d), lambda i: (i, 0)),
                  pl.BlockSpec((1, d), lambda i: (0, 0))],
        out_specs=pl.BlockSpec((tm, d), lambda i: (i, 0)),
        out_shape=jax.ShapeDtypeStruct((t, d), BF16),
        compiler_params=_cparams("parallel"),
        name="prenorm",
    )(x, g)


def _rope(xh, cos, sin_lo, sin_hi):
    half = ROPE_DIM // 2
    return (xh * cos + pltpu.roll(xh, LANES - half, 1) * sin_lo
            + pltpu.roll(xh, half, 1) * sin_hi)


def _qkv_kernel(xn_ref, w_ref, cos_ref, slo_ref, shi_ref, o_ref, km_ref):
    j = pl.program_id(1)
    y = _dot(xn_ref[...], w_ref[...])
    tm = y.shape[0]

    @pl.when(j == 2)
    def _():
        o_ref[...] = y.astype(BF16)

    @pl.when(j < 2)
    def _():
        cos, slo, shi = cos_ref[...], slo_ref[...], shi_ref[...]
        scale = jnp.where(j == 0, F32(HEAD_DIM ** -0.5), F32(1.0))
        for h in range(N_HEADS):
            sl = slice(h * HEAD_DIM, (h + 1) * HEAD_DIM)
            r = _rope(y[:, sl], cos, slo, shi)
            o_ref[:, sl] = (r * scale).astype(BF16)

            @pl.when(j == 1)
            def _():
                for c in range(tm // MOBA_BLOCK):
                    blk = r[c * MOBA_BLOCK:(c + 1) * MOBA_BLOCK]
                    km_ref[0, c:c + 1, sl] = jnp.mean(blk, axis=0, keepdims=True)


def _qkv(xn, w_qkv, cos, slo, shi, seq, tm=512):
    t, d = xn.shape
    tiles_per_seq = seq // tm
    rope_spec = pl.BlockSpec((tm, LANES), lambda i, j: (i % tiles_per_seq, 0))
    return pl.pallas_call(
        _qkv_kernel,
        grid=(t // tm, 3),
        in_specs=[pl.BlockSpec((tm, d), lambda i, j: (i, 0)),
                  pl.BlockSpec((d, ATTN_WIDTH), lambda i, j: (0, j)),
                  rope_spec, rope_spec, rope_spec],
        out_specs=[pl.BlockSpec((tm, ATTN_WIDTH), lambda i, j: (i, j)),
                   pl.BlockSpec((1, tm // MOBA_BLOCK, ATTN_WIDTH), lambda i, j: (i, 0, 0))],
        out_shape=[jax.ShapeDtypeStruct((t, 3 * ATTN_WIDTH), BF16),
                   jax.ShapeDtypeStruct((t // tm, tm // MOBA_BLOCK, ATTN_WIDTH), F32)],
        compiler_params=_cparams("parallel", "arbitrary"),
        name="qkv_proj",
    )(xn, w_qkv, cos, slo, shi)


def _convmod_kernel(xn_ref, xh_ref, wa_ref, wg_ref, dww_ref, dwb_ref, lng_ref, lnb_ref,
                    o_ref, a_ref, *, tiles_per_seq):
    i = pl.program_id(0)
    tm = xn_ref.shape[0]

    def glu(xn):
        return _dot(xn, wa_ref[...]) * _sigmoid(_dot(xn, wg_ref[...]))

    halo = glu(xh_ref[...])
    a_ref[0:CONV_HALO, :] = jnp.where(i % tiles_per_seq == 0, 0.0, halo)
    a_ref[CONV_HALO:, :] = glu(xn_ref[...])

    rb = 128
    first = CONV_HALO - (CONV_KERNEL - 1)
    for r in range(tm // rb):
        cols = []
        for c in range(CONV_CHANNELS // LANES):
            cs = slice(c * LANES, (c + 1) * LANES)
            acc = jnp.zeros((rb, LANES), F32)
            for k in range(CONV_KERNEL):
                acc = acc + dww_ref[k:k + 1, cs] * a_ref[pl.ds(r * rb + first + k, rb), cs]
            cols.append(acc + dwb_ref[:, cs])
        y = jnp.concatenate(cols, axis=1)
        y = _layer_norm(y, lng_ref[...], lnb_ref[...])
        o_ref[r * rb:(r + 1) * rb, :] = (y * _sigmoid(y)).astype(BF16)


def _convmod(xn, w_ca, w_cg, dw_w, dw_b, ln_g, ln_b, seq, tm=512):
    t, d = xn.shape
    c = CONV_CHANNELS
    ratio = tm // CONV_HALO
    const = lambda i: (0, 0)
    return pl.pallas_call(
        functools.partial(_convmod_kernel, tiles_per_seq=seq // tm),
        grid=(t // tm,),
        in_specs=[pl.BlockSpec((tm, d), lambda i: (i, 0)),
                  pl.BlockSpec((CONV_HALO, d), lambda i: (jnp.maximum(i * ratio - 1, 0), 0)),
                  pl.BlockSpec((d, c), const), pl.BlockSpec((d, c), const),
                  pl.BlockSpec((CONV_KERNEL, c), const), pl.BlockSpec((1, c), const),
                  pl.BlockSpec((1, c), const), pl.BlockSpec((1, c), const)],
        out_specs=pl.BlockSpec((tm, c), lambda i: (i, 0)),
        out_shape=jax.ShapeDtypeStruct((t, c), BF16),
        scratch_shapes=[pltpu.VMEM((tm + CONV_HALO, c), F32)],
        compiler_params=_cparams("parallel"),
        name="conv_module",
    )(xn, xn, w_ca, w_cg, dw_w, dw_b, ln_g, ln_b)


def _sgu_kernel(xn_ref, wu_ref, wv_ref, lng_ref, lnb_ref, sw_ref, sbt_ref, o_ref):
    tm = xn_ref.shape[0]
    xn = xn_ref[...]
    u = _gelu_tanh(_dot(xn, wu_ref[...]))
    v = _layer_norm(_gelu_tanh(_dot(xn, wv_ref[...])), lng_ref[...], lnb_ref[...]).astype(BF16)
    row = lax.broadcasted_iota(jnp.int32, (SGU_CHUNK, SGU_CHUNK), 0)
    col = lax.broadcasted_iota(jnp.int32, (SGU_CHUNK, SGU_CHUNK), 1)
    for g in range(SGU_GROUPS):
        gs = slice(g * SGU_GROUP_DIM, (g + 1) * SGU_GROUP_DIM)
        w = jnp.where(col <= row, sw_ref[g], 0.0).astype(BF16)
        bias = sbt_ref[:, g:g + 1]
        for n in range(tm // SGU_CHUNK):
            rs = slice(n * SGU_CHUNK, (n + 1) * SGU_CHUNK)
            mixed = _dot(w, v[rs, gs]) + bias
            o_ref[rs, gs] = (u[rs, gs] * mixed).astype(BF16)


def _sgu(xn, w_su, w_sv, ln_g, ln_b, sgu_w, sgu_bt, tm=512):
    t, d = xn.shape
    c = SGU_WIDTH
    const = lambda i: (0, 0)
    return pl.pallas_call(
        _sgu_kernel,
        grid=(t // tm,),
        in_specs=[pl.BlockSpec((tm, d), lambda i: (i, 0)),
                  pl.BlockSpec((d, c), const), pl.BlockSpec((d, c), const),
                  pl.BlockSpec((1, c), const), pl.BlockSpec((1, c), const),
                  pl.BlockSpec((SGU_GROUPS, SGU_CHUNK, SGU_CHUNK), lambda i: (0, 0, 0)),
                  pl.BlockSpec((SGU_CHUNK, SGU_GROUPS), const)],
        out_specs=pl.BlockSpec((tm, c), lambda i: (i, 0)),
        out_shape=jax.ShapeDtypeStruct((t, c), BF16),
        compiler_params=_cparams("parallel"),
        name="sgu",
    )(xn, w_su, w_sv, ln_g, ln_b, sgu_w, sgu_bt)


def _gates_kernel(xn_ref, w_ref, o_ref):
    o_ref[...] = _sigmoid(_dot(xn_ref[...], w_ref[...])).astype(BF16)


def _gates(xn, w, tm=512, tn=1024):
    t, d = xn.shape
    n = w.shape[1]
    return pl.pallas_call(
        _gates_kernel,
        grid=(t // tm, n // tn),
        in_specs=[pl.BlockSpec((tm, d), lambda i, j: (i, 0)),
                  pl.BlockSpec((d, tn), lambda i, j: (0, j))],
        out_specs=pl.BlockSpec((tm, tn), lambda i, j: (i, j)),
        out_shape=jax.ShapeDtypeStruct((t, n), BF16),
        compiler_params=_cparams("parallel", "arbitrary"),
        name="gates",
    )(xn, w)


def _moba_kernel(q_ref, k_ref, v_ref, km_ref, o_ref):
    own = pl.program_id(2)
    blk = MOBA_BLOCK
    q = q_ref[0]
    nbp = km_ref.shape[1]

    gate = _dot_nt(q, km_ref[0])
    col = lax.broadcasted_iota(jnp.int32, gate.shape, 1)
    colf = col.astype(F32)
    past = col < own
    g = jnp.where(past, gate, -jnp.inf)
    sel = jnp.zeros(gate.shape, jnp.bool_)
    for _ in range(MOBA_TOPK):
        top = jnp.max(g, axis=1, keepdims=True)
        first = jnp.min(jnp.where(g == top, colf, F32(nbp)), axis=1, keepdims=True)
        pick = colf == first
        sel = jnp.logical_or(sel, pick)
        g = jnp.where(pick, -jnp.inf, g)
    sel = jnp.logical_and(sel, past)
    q_aug = jnp.concatenate([q, jnp.where(sel, 0.0, MASK_VALUE).astype(BF16)], axis=1)

    start = pl.multiple_of(own * blk, blk)
    s = _dot_nt(q, k_ref[0, pl.ds(start, blk), :])
    r_i = lax.broadcasted_iota(jnp.int32, s.shape, 0)
    c_i = lax.broadcasted_iota(jnp.int32, s.shape, 1)
    s = jnp.where(c_i <= r_i, s, MASK_VALUE)
    m = jnp.max(s, axis=1, keepdims=True)
    p = jnp.exp(s - m)
    l = jnp.sum(p, axis=1, keepdims=True)
    acc = _dot(p.astype(BF16), v_ref[0, pl.ds(start, blk), :])

    def body(j, carry):
        m, l, acc = carry
        st = pl.multiple_of(j * blk, blk)
        onehot = (lax.broadcasted_iota(jnp.int32, (blk, nbp), 1) == j).astype(BF16)
        k_aug = jnp.concatenate([k_ref[0, pl.ds(st, blk), :], onehot], axis=1)
        s = _dot_nt(q_aug, k_aug)
        m_new = jnp.maximum(m, jnp.max(s, axis=1, keepdims=True))
        alpha = jnp.exp(m - m_new)
        p = jnp.exp(s - m_new)
        l = alpha * l + jnp.sum(p, axis=1, keepdims=True)
        acc = alpha * acc + _dot(p.astype(BF16), v_ref[0, pl.ds(st, blk), :])
        return m_new, l, acc

    m, l, acc = lax.fori_loop(0, own, body, (m, l, acc))
    o_ref[0] = (acc / l).astype(BF16)


def _moba(qkv, kmean):
    b, s, _ = qkv.shape
    nbp = kmean.shape[1]
    blk = MOBA_BLOCK
    return pl.pallas_call(
        _moba_kernel,
        grid=(b, N_HEADS, s // blk),
        in_specs=[pl.BlockSpec((1, blk, HEAD_DIM), lambda bi, h, qi: (bi, qi, h)),
                  pl.BlockSpec((1, s, HEAD_DIM), lambda bi, h, qi: (bi, 0, N_HEADS + h)),
                  pl.BlockSpec((1, s, HEAD_DIM), lambda bi, h, qi: (bi, 0, 2 * N_HEADS + h)),
                  pl.BlockSpec((1, nbp, HEAD_DIM), lambda bi, h, qi: (bi, 0, h))],
        out_specs=pl.BlockSpec((1, blk, HEAD_DIM), lambda bi, h, qi: (bi, qi, h)),
        out_shape=jax.ShapeDtypeStruct((b, s, ATTN_WIDTH), BF16),
        compiler_params=_cparams("parallel", "parallel", "arbitrary"),
        name="moba_attention",
    )(qkv, qkv, qkv, kmean)


def _merge_kernel(x_ref, a_ref, sb_ref, at_ref, gt_ref, wca_ref, wsg_ref, wat_ref, wo_ref,
                  gpost_ref, o_ref):
    d = D_MODEL
    merged = (gt_ref[:, 0:d].astype(F32) * _dot(a_ref[...], wca_ref[...])
              + gt_ref[:, d:2 * d].astype(F32) * _dot(sb_ref[...], wsg_ref[...])
              + gt_ref[:, 2 * d:3 * d].astype(F32) * _dot(at_ref[...], wat_ref[...]))
    h = _dot(merged.astype(BF16), wo_ref[...])
    o_ref[...] = x_ref[...] + _rms(h, gpost_ref[...])


def _merge(x, a_act, sb, attn, gates, conv_out, sgu_out, attn_out, w_o, g_post, tm=256):
    t, d = x.shape
    const = lambda i: (0, 0)
    wspec = lambda k: pl.BlockSpec((k, d), const, pipeline_mode=pl.Buffered(1))
    return pl.pallas_call(
        _merge_kernel,
        grid=(t // tm,),
        in_specs=[pl.BlockSpec((tm, d), lambda i: (i, 0)),
                  pl.BlockSpec((tm, CONV_CHANNELS), lambda i: (i, 0)),
                  pl.BlockSpec((tm, SGU_WIDTH), lambda i: (i, 0)),
                  pl.BlockSpec((tm, ATTN_WIDTH), lambda i: (i, 0)),
                  pl.BlockSpec((tm, 3 * d), lambda i: (i, 0)),
                  wspec(CONV_CHANNELS), wspec(SGU_WIDTH), wspec(ATTN_WIDTH), wspec(d),
                  pl.BlockSpec((1, d), const)],
        out_specs=pl.BlockSpec((tm, d), lambda i: (i, 0)),
        out_shape=jax.ShapeDtypeStruct((t, d), F32),
        compiler_params=_cparams("parallel"),
        name="merge",
    )(x, a_act, sb, attn, gates, conv_out, sgu_out, attn_out, w_o, g_post)


def _ffn_kernel(x_ref, xh_ref, gpre_ref, wa_ref, wb_ref, dww_ref, dwb_ref, wo_ref, gpost_ref,
                o_ref, hn_ref, acc_ref, *, tiles_per_seq):
    i = pl.program_id(0)
    j = pl.program_id(1)
    tm = x_ref.shape[0]

    @pl.when(j == 0)
    def _():
        halo = _rms(xh_ref[...], gpre_ref[...])
        hn_ref[0:FFN_HALO, :] = jnp.where(i % tiles_per_seq == 0, 0.0, halo).astype(BF16)
        hn_ref[FFN_HALO:, :] = _rms(x_ref[...], gpre_ref[...]).astype(BF16)
        acc_ref[...] = jnp.zeros_like(acc_ref)

    a = _dot(hn_ref[...], wa_ref[...])
    b = _dot(hn_ref[FFN_HALO:, :], wb_ref[...])
    conv = dwb_ref[...] + dww_ref[2:3, :] * a[FFN_HALO:]
    for k in (1, 2):
        conv = conv + dww_ref[2 - k:3 - k, :] * pltpu.roll(a, k, 0)[FFN_HALO:]
    act = (_gelu_tanh(conv) * b).astype(BF16)
    acc_ref[...] += _dot(act, wo_ref[...])

    @pl.when(j == pl.num_programs(1) - 1)
    def _():
        o_ref[...] = x_ref[...] + _rms(acc_ref[...], gpost_ref[...])


def _ffn(x, g_pre, ffn_in, dw_w, dw_b, ffn_out, g_post, seq, tm=512, tf=512):
    t, d = x.shape
    nf = D_FF // tf
    ratio = tm // FFN_HALO
    return pl.pallas_call(
        functools.partial(_ffn_kernel, tiles_per_seq=seq // tm),
        grid=(t // tm, nf),
        in_specs=[pl.BlockSpec((tm, d), lambda i, j: (i, 0)),
                  pl.BlockSpec((FFN_HALO, d), lambda i, j: (jnp.maximum(i * ratio - 1, 0), 0)),
                  pl.BlockSpec((1, d), lambda i, j: (0, 0)),
                  pl.BlockSpec((d, tf), lambda i, j: (0, j)),
                  pl.BlockSpec((d, tf), lambda i, j: (0, j + nf)),
                  pl.BlockSpec((FFN_CONV_KERNEL, tf), lambda i, j: (0, j)),
                  pl.BlockSpec((1, tf), lambda i, j: (0, j)),
                  pl.BlockSpec((tf, d), lambda i, j: (j, 0)),
                  pl.BlockSpec((1, d), lambda i, j: (0, 0))],
        out_specs=pl.BlockSpec((tm, d), lambda i, j: (i, 0)),
        out_shape=jax.ShapeDtypeStruct((t, d), F32),
        scratch_shapes=[pltpu.VMEM((tm + FFN_HALO, d), BF16), pltpu.VMEM((tm, d), F32)],
        compiler_params=_cparams("parallel", "arbitrary"),
        name="conv_ffn",
    )(x, x, g_pre, ffn_in, ffn_in, dw_w, dw_b, ffn_out, g_post)


def _ple_kernel(x_ref, p_ref, g_ref, wg_ref, wp_ref, o_ref):
    x = x_ref[...]
    gate = _sigmoid(_dot(_rms(x, g_ref[...]).astype(BF16), wg_ref[...]))
    o_ref[...] = x + gate * _dot(p_ref[...].astype(BF16), wp_ref[...])


def _ple(x, p, g, w_gate, w_proj, tm=512):
    t, d = x.shape
    const = lambda i: (0, 0)
    return pl.pallas_call(
        _ple_kernel,
        grid=(t // tm,),
        in_specs=[pl.BlockSpec((tm, d), lambda i: (i, 0)),
                  pl.BlockSpec((tm, PLE_DIM), lambda i: (i, 0)),
                  pl.BlockSpec((1, d), const),
                  pl.BlockSpec((d, d), const, pipeline_mode=pl.Buffered(1)),
                  pl.BlockSpec((PLE_DIM, d), const, pipeline_mode=pl.Buffered(1))],
        out_specs=pl.BlockSpec((tm, d), lambda i: (i, 0)),
        out_shape=jax.ShapeDtypeStruct((t, d), F32),
        compiler_params=_cparams("parallel"),
        name="ple",
    )(x, p, g, w_gate, w_proj)


def _rope_tables(seq):
    half = ROPE_DIM // 2
    inv = F32(ROPE_THETA) ** (-jnp.arange(0, ROPE_DIM, 2, dtype=F32) / ROPE_DIM)
    ang = jnp.arange(seq, dtype=jnp.int32).astype(F32)[:, None] * inv[None, :]
    cos, sin = jnp.cos(ang), jnp.sin(ang)
    zeros = jnp.zeros((seq, LANES - ROPE_DIM), F32)
    z16 = jnp.zeros((seq, half), F32)
    cos_t = jnp.concatenate([cos, cos, jnp.ones((seq, LANES - ROPE_DIM), F32)], axis=1)
    sin_lo = jnp.concatenate([-sin, z16, zeros], axis=1)
    sin_hi = jnp.concatenate([z16, sin, zeros], axis=1)
    return cos_t, sin_lo, sin_hi


def kernel(x, p, mix_norm_pre, mix_norm_post, w_in, conv_dw_w, conv_dw_b, conv_norm_g, conv_norm_b, conv_out, sgu_norm_g, sgu_norm_b, sgu_w, sgu_b, sgu_out, attn_out, w_o, ffn_norm_pre, ffn_norm_post, ffn_in, ffn_dw_w, ffn_dw_b, ffn_out, ple_norm, ple_gate, ple_proj):
    bsz, seq, d = x.shape
    depth = w_in.shape[0]
    assert d == D_MODEL and seq % 512 == 0
    t = bsz * seq
    nb = seq // MOBA_BLOCK
    nbp = LANES * pl.cdiv(nb, LANES)
    assert nbp == LANES, "block-mask columns share one 256-deep contraction with the head dim"

    cos_t, sin_lo, sin_hi = _rope_tables(seq)
    row = lambda v: v.reshape(1, -1)
    c0 = 3 * ATTN_WIDTH
    c1 = c0 + CONV_CHANNELS
    c2 = c1 + CONV_CHANNELS
    c3 = c2 + SGU_WIDTH
    c4 = c3 + SGU_WIDTH

    xf = x.reshape(t, d)
    for i in range(depth):
        w = w_in[i].astype(BF16)
        xn = _prenorm(xf, row(mix_norm_pre[i]))
        qkv, kmean = _qkv(xn, w[:, :c0], cos_t, sin_lo, sin_hi, seq)
        a_act = _convmod(xn, w[:, c0:c1], w[:, c1:c2], conv_dw_w[i], row(conv_dw_b[i]),
                         row(conv_norm_g[i]), row(conv_norm_b[i]), seq)
        sb = _sgu(xn, w[:, c2:c3], w[:, c3:c4], row(sgu_norm_g[i]), row(sgu_norm_b[i]),
                  sgu_w[i], sgu_b[i].T)
        gates = _gates(xn, w[:, c4:])
        kmean = kmean.reshape(bsz, nb, ATTN_WIDTH)
        kmean = jnp.pad(kmean, ((0, 0), (0, nbp - nb), (0, 0))).astype(BF16)
        attn = _moba(qkv.reshape(bsz, seq, 3 * ATTN_WIDTH), kmean).reshape(t, ATTN_WIDTH)
        xf = _merge(xf, a_act, sb, attn, gates, conv_out[i].astype(BF16), sgu_out[i].astype(BF16),
                    attn_out[i].astype(BF16), w_o[i].astype(BF16), row(mix_norm_post[i]))
        xf = _ffn(xf, row(ffn_norm_pre[i]), ffn_in[i].astype(BF16), ffn_dw_w[i], row(ffn_dw_b[i]),
                  ffn_out[i].astype(BF16), row(ffn_norm_post[i]), seq)
        xf = _ple(xf, p[i].reshape(t, PLE_DIM), row(ple_norm[i]), ple_gate[i].astype(BF16),
                  ple_proj[i].astype(BF16))
    return xf.reshape(bsz, seq, d)
```

```python
import functools

import jax
import jax.numpy as jnp
from jax import lax
from jax.experimental import pallas as pl
from jax.experimental.pallas import tpu as pltpu

F32 = jnp.float32
BF16 = jnp.bfloat16

D_MODEL = 2048
PLE_DIM = 256
N_HEADS = 8
HEAD_DIM = 128
ATTN_WIDTH = N_HEADS * HEAD_DIM
ROPE_DIM = HEAD_DIM // 4
ROPE_THETA = 500000.0
MOBA_BLOCK = 256
MOBA_TOPK = 3
CONV_CHANNELS = 512
CONV_KERNEL = 31
SGU_GROUPS = 4
SGU_GROUP_DIM = 128
SGU_WIDTH = SGU_GROUPS * SGU_GROUP_DIM
SGU_CHUNK = 128
D_FF = 5632
FFN_CONV_KERNEL = 3
EPS = 1e-6

VMEM_LIMIT_BYTES = 56 * 1024 * 1024
LANES = 128
BF16_SUBLANES = 16
MASK_VALUE = -1e30

CONV_HALO = 32
FFN_HALO = BF16_SUBLANES
VT_ROWS = HEAD_DIM + BF16_SUBLANES
Q_SCALE = HEAD_DIM ** -0.5 * 1.4426950408889634
GROUP_BLOCKS = 4
KEY_GROUP = GROUP_BLOCKS * MOBA_BLOCK


def _cparams(*sem):
    return pltpu.CompilerParams(dimension_semantics=sem, vmem_limit_bytes=VMEM_LIMIT_BYTES)


def _rms(x, g):
    return x * lax.rsqrt(jnp.mean(x * x, axis=-1, keepdims=True) + EPS) * g


def _layer_norm(x, g, b):
    mu = jnp.mean(x, axis=-1, keepdims=True)
    xc = x - mu
    var = jnp.mean(xc * xc, axis=-1, keepdims=True)
    return xc * lax.rsqrt(var + EPS) * g + b


def _gelu_tanh(x):
    c = 0.7978845608028654
    return 0.5 * x * (1.0 + jnp.tanh(c * (x + 0.044715 * (x * x * x))))


def _sigmoid(x):
    return 1.0 / (1.0 + jnp.exp(-x))


def _dot(a, b):
    return jnp.dot(a, b, preferred_element_type=F32)


def _dot_nt(a, b):
    return lax.dot_general(a, b, (((1,), (1,)), ((), ())), preferred_element_type=F32)


def _prenorm_kernel(x_ref, g_ref, o_ref):
    o_ref[...] = _rms(x_ref[...], g_ref[...]).astype(BF16)


def _prenorm(x, g, tm=512):
    t, d = x.shape
    return pl.pallas_call(
        _prenorm_kernel,
        grid=(t // tm,),
        in_specs=[pl.BlockSpec((tm, d), lambda i: (i, 0)),
                  pl.BlockSpec((1, d), lambda i: (0, 0))],
        out_specs=pl.BlockSpec((tm, d), lambda i: (i, 0)),
        out_shape=jax.ShapeDtypeStruct((t, d), BF16),
        compiler_params=_cparams("parallel"),
        name="prenorm",
    )(x, g)


def _rope(xh, cos, sin_lo, sin_hi):
    half = ROPE_DIM // 2
    return (xh * cos + pltpu.roll(xh, LANES - half, 1) * sin_lo
            + pltpu.roll(xh, half, 1) * sin_hi)


def _qkv_kernel(xn_ref, w_ref, cos_ref, slo_ref, shi_ref, q_ref, k_ref, vt_ref, km_ref,
                *, tiles_per_seq):
    i = pl.program_id(0)
    j = pl.program_id(1)
    y = _dot(xn_ref[...], w_ref[...])
    tm = y.shape[0]
    nblk = tm // MOBA_BLOCK
    heads = [slice(h * HEAD_DIM, (h + 1) * HEAD_DIM) for h in range(N_HEADS)]

    @pl.when(j == 0)
    def _():
        cos, slo, shi = cos_ref[...], slo_ref[...], shi_ref[...]
        for sl in heads:
            q_ref[:, sl] = (_rope(y[:, sl], cos, slo, shi) * F32(Q_SCALE)).astype(BF16)

    @pl.when(j == 1)
    def _():
        cos, slo, shi = cos_ref[...], slo_ref[...], shi_ref[...]
        row = lax.broadcasted_iota(jnp.int32, (tm, LANES), 0)
        lane = lax.broadcasted_iota(jnp.int32, (tm, LANES), 1)
        first_blk = (i % tiles_per_seq) * nblk
        onehot = jnp.zeros((tm, LANES), F32)
        for c in range(nblk):
            in_c = jnp.logical_and(row >= c * MOBA_BLOCK, row < (c + 1) * MOBA_BLOCK)
            onehot = jnp.where(jnp.logical_and(in_c, lane == first_blk + c), 1.0, onehot)
        onehot = onehot.astype(BF16)
        for h, sl in enumerate(heads):
            r = _rope(y[:, sl], cos, slo, shi)
            k_ref[:, 2 * h * HEAD_DIM:(2 * h + 1) * HEAD_DIM] = r.astype(BF16)
            k_ref[:, (2 * h + 1) * HEAD_DIM:(2 * h + 2) * HEAD_DIM] = onehot
            for c in range(nblk):
                blk = r[c * MOBA_BLOCK:(c + 1) * MOBA_BLOCK]
                km_ref[0, c:c + 1, sl] = jnp.mean(blk, axis=0, keepdims=True)

    @pl.when(j == 2)
    def _():
        for h, sl in enumerate(heads):
            vt_ref[h, 0, 0:HEAD_DIM, :] = y[:, sl].T.astype(BF16)
            vt_ref[h, 0, HEAD_DIM:, :] = jnp.ones((VT_ROWS - HEAD_DIM, tm), BF16)


def _qkv(xn, w_qkv, cos, slo, shi, seq, tm=512):
    t, d = xn.shape
    tiles_per_seq = seq // tm
    nblk = tm // MOBA_BLOCK
    per_group = KEY_GROUP // tm
    assert per_group * tm == KEY_GROUP
    rope_spec =pl.BlockSpec((tm, LANES), lambda i, j: (i % tiles_per_seq, 0))
    return pl.pallas_call(
        functools.partial(_qkv_kernel, tiles_per_seq=tiles_per_seq),
        grid=(t // tm, 3),
        in_specs=[pl.BlockSpec((tm, d), lambda i, j: (i, 0)),
                  pl.BlockSpec((d, ATTN_WIDTH), lambda i, j: (0, j)),
                  rope_spec, rope_spec, rope_spec],
        out_specs=[pl.BlockSpec((tm, ATTN_WIDTH), lambda i, j: (i, 0)),
                   pl.BlockSpec((tm, 2 * ATTN_WIDTH), lambda i, j: (i, 0)),
                   pl.BlockSpec((N_HEADS, 1, VT_ROWS, tm), lambda i, j: (0, i // per_group, 0, i % per_group)),
                   pl.BlockSpec((1, nblk, ATTN_WIDTH), lambda i, j: (i, 0, 0))],
        out_shape=[jax.ShapeDtypeStruct((t, ATTN_WIDTH), BF16),
                   jax.ShapeDtypeStruct((t, 2 * ATTN_WIDTH), BF16),
                   jax.ShapeDtypeStruct((N_HEADS, t // KEY_GROUP, VT_ROWS, KEY_GROUP), BF16),
                   jax.ShapeDtypeStruct((t // tm, nblk, ATTN_WIDTH), F32)],
        compiler_params=_cparams("parallel", "arbitrary"),
        name="qkv_proj",
    )(xn, w_qkv, cos, slo, shi)


def _convmod_kernel(xn_ref, xh_ref, wa_ref, wg_ref, dww_ref, dwb_ref, lng_ref, lnb_ref,
                    o_ref, a_ref, *, tiles_per_seq):
    i = pl.program_id(0)
    tm = xn_ref.shape[0]

    def glu(xn):
        return _dot(xn, wa_ref[...]) * _sigmoid(_dot(xn, wg_ref[...]))

    halo = glu(xh_ref[...])
    a_ref[0:CONV_HALO, :] = jnp.where(i % tiles_per_seq == 0, 0.0, halo)
    a_ref[CONV_HALO:, :] = glu(xn_ref[...])

    rb = 128
    first = CONV_HALO - (CONV_KERNEL - 1)
    for r in range(tm // rb):
        cols = []
        for c in range(CONV_CHANNELS // LANES):
            cs = slice(c * LANES, (c + 1) * LANES)
            acc = jnp.zeros((rb, LANES), F32)
            for k in range(CONV_KERNEL):
                acc = acc + dww_ref[k:k + 1, cs] * a_ref[pl.ds(r * rb + first + k, rb), cs]
            cols.append(acc + dwb_ref[:, cs])
        y = jnp.concatenate(cols, axis=1)
        y = _layer_norm(y, lng_ref[...], lnb_ref[...])
        o_ref[r * rb:(r + 1) * rb, :] = (y * _sigmoid(y)).astype(BF16)


def _convmod(xn, w_ca, w_cg, dw_w, dw_b, ln_g, ln_b, seq, tm=512):
    t, d = xn.shape
    c = CONV_CHANNELS
    ratio = tm // CONV_HALO
    const = lambda i: (0, 0)
    return pl.pallas_call(
        functools.partial(_convmod_kernel, tiles_per_seq=seq // tm),
        grid=(t // tm,),
        in_specs=[pl.BlockSpec((tm, d), lambda i: (i, 0)),
                  pl.BlockSpec((CONV_HALO, d), lambda i: (jnp.maximum(i * ratio - 1, 0), 0)),
                  pl.BlockSpec((d, c), const), pl.BlockSpec((d, c), const),
                  pl.BlockSpec((CONV_KERNEL, c), const), pl.BlockSpec((1, c), const),
                  pl.BlockSpec((1, c), const), pl.BlockSpec((1, c), const)],
        out_specs=pl.BlockSpec((tm, c), lambda i: (i, 0)),
        out_shape=jax.ShapeDtypeStruct((t, c), BF16),
        scratch_shapes=[pltpu.VMEM((tm + CONV_HALO, c), F32)],
        compiler_params=_cparams("parallel"),
        name="conv_module",
    )(xn, xn, w_ca, w_cg, dw_w, dw_b, ln_g, ln_b)


def _sgu_kernel(xn_ref, wu_ref, wv_ref, lng_ref, lnb_ref, sw_ref, sbt_ref, o_ref):
    tm = xn_ref.shape[0]
    xn = xn_ref[...]
    u = _gelu_tanh(_dot(xn, wu_ref[...]))
    v = _layer_norm(_gelu_tanh(_dot(xn, wv_ref[...])), lng_ref[...], lnb_ref[...]).astype(BF16)
    row = lax.broadcasted_iota(jnp.int32, (SGU_CHUNK, SGU_CHUNK), 0)
    col = lax.broadcasted_iota(jnp.int32, (SGU_CHUNK, SGU_CHUNK), 1)
    for g in range(SGU_GROUPS):
        gs = slice(g * SGU_GROUP_DIM, (g + 1) * SGU_GROUP_DIM)
        w = jnp.where(col <= row, sw_ref[g], 0.0).astype(BF16)
        bias = sbt_ref[:, g:g + 1]
        for n in range(tm // SGU_CHUNK):
            rs = slice(n * SGU_CHUNK, (n + 1) * SGU_CHUNK)
            mixed = _dot(w, v[rs, gs]) + bias
            o_ref[rs, gs] = (u[rs, gs] * mixed).astype(BF16)


def _sgu(xn, w_su, w_sv, ln_g, ln_b, sgu_w, sgu_bt, tm=512):
    t, d = xn.shape
    c = SGU_WIDTH
    const = lambda i: (0, 0)
    return pl.pallas_call(
        _sgu_kernel,
        grid=(t // tm,),
        in_specs=[pl.BlockSpec((tm, d), lambda i: (i, 0)),
                  pl.BlockSpec((d, c), const), pl.BlockSpec((d, c), const),
                  pl.BlockSpec((1, c), const), pl.BlockSpec((1, c), const),
                  pl.BlockSpec((SGU_GROUPS, SGU_CHUNK, SGU_CHUNK), lambda i: (0, 0, 0)),
                  pl.BlockSpec((SGU_CHUNK, SGU_GROUPS), const)],
        out_specs=pl.BlockSpec((tm, c), lambda i: (i, 0)),
        out_shape=jax.ShapeDtypeStruct((t, c), BF16),
        compiler_params=_cparams("parallel"),
        name="sgu",
    )(xn, w_su, w_sv, ln_g, ln_b, sgu_w, sgu_bt)


def _gates_kernel(xn_ref, w_ref, o_ref):
    o_ref[...] = _sigmoid(_dot(xn_ref[...], w_ref[...])).astype(BF16)


def _gates(xn, w, tm=512, tn=1024):
    t, d = xn.shape
    n = w.shape[1]
    return pl.pallas_call(
        _gates_kernel,
        grid=(t // tm, n // tn),
        in_specs=[pl.BlockSpec((tm, d), lambda i, j: (i, 0)),
                  pl.BlockSpec((d, tn), lambda i, j: (0, j))],
        out_specs=pl.BlockSpec((tm, tn), lambda i, j: (i, j)),
        out_shape=jax.ShapeDtypeStruct((t, n), BF16),
        compiler_params=_cparams("parallel", "arbitrary"),
        name="gates",
    )(xn, w)


def _moba_kernel(q_ref, k_ref, vt_ref, km_ref, o_ref, qa_ref, s_ref, p_ref):
    qi = pl.program_id(2)
    qt = q_ref.shape[1]
    q_blocks = qt // MOBA_BLOCK
    q = q_ref[0]
    nbp = km_ref.shape[1]

    gate = _dot_nt(q, km_ref[0])
    col = lax.broadcasted_iota(jnp.int32, gate.shape, 1)
    row = lax.broadcasted_iota(jnp.int32, gate.shape, 0)
    own = qi * q_blocks + sum((row >= r * MOBA_BLOCK).astype(jnp.int32) for r in range(1, q_blocks))
    colf = col.astype(F32)
    past = col < own
    g = jnp.where(past, gate, -jnp.inf)
    sel = jnp.zeros(gate.shape, jnp.bool_)
    for _ in range(MOBA_TOPK):
        top = jnp.max(g, axis=1, keepdims=True)
        first = jnp.min(jnp.where(g == top, colf, F32(nbp)), axis=1, keepdims=True)
        pick = colf == first
        sel = jnp.logical_or(sel, pick)
        g = jnp.where(pick, -jnp.inf, g)
    allowed = jnp.logical_or(jnp.logical_and(sel, past), col == own)
    qa_ref[0] = jnp.concatenate([q, jnp.where(allowed, 0.0, MASK_VALUE).astype(BF16)], axis=1)
    qa_ref[1] = jnp.concatenate([q, jnp.full(gate.shape, MASK_VALUE, BF16)], axis=1)

    own_g = (qi * q_blocks) // GROUP_BLOCKS
    pad_t = jnp.where(own_g % 2 == 0, own_g, -1)
    n_all = own_g + (1 - own_g % 2) + 1

    def qk_stage(t, slot):
        g = jnp.minimum(t, own_g)
        st = pl.multiple_of(g * KEY_GROUP, KEY_GROUP)
        which = (t == pad_t).astype(jnp.int32)
        s_ref[slot] = _dot_nt(k_ref[0, pl.ds(st, KEY_GROUP), :], qa_ref[which])

    def softmax_stage(slot, m, causal=False):
        s = s_ref[slot]
        if causal:
            key_i = lax.broadcasted_iota(jnp.int32, s.shape, 0)
            qry_i = (lax.broadcasted_iota(jnp.int32, s.shape, 1)
                     + ((qi * q_blocks) % GROUP_BLOCKS) * MOBA_BLOCK)
            s = jnp.where(key_i <= qry_i, s, MASK_VALUE)
        m_new = jnp.maximum(m, jnp.max(s, axis=0, keepdims=True))
        p_ref[slot] = jnp.exp2(s - m_new).astype(BF16)
        return m_new, jnp.exp2(m - m_new)

    def pv_stage(t, slot, acc, alpha):
        return alpha * acc + _dot(vt_ref[0, jnp.minimum(t, own_g)], p_ref[slot])

    m = jnp.full((1, qt), MASK_VALUE, F32)
    acc = jnp.zeros((VT_ROWS, qt), F32)
    qk_stage(0, 0)
    qk_stage(1, 1)
    m, a0 = softmax_stage(0, m)

    def pair(u, carry):
        m, acc, a0 = carry
        t = 2 + 2 * u
        qk_stage(t, 0)
        m, a1 = softmax_stage(1, m)
        acc = pv_stage(t - 2, 0, acc, a0)
        qk_stage(t + 1, 1)
        m, a0 = softmax_stage(0, m)
        acc = pv_stage(t - 1, 1, acc, a1)
        return m, acc, a0

    m, acc, a0 = lax.fori_loop(0, (n_all - 2) // 2, pair, (m, acc, a0))
    m, a1 = softmax_stage(1, m, causal=True)
    acc = pv_stage(n_all - 2, 0, acc, a0)
    acc = pv_stage(n_all - 1, 1, acc, a1)
    out_t = acc[0:HEAD_DIM] / acc[HEAD_DIM:HEAD_DIM + 1]
    o_ref[0] = out_t.T.astype(BF16)


def _moba(q, k_aug, vt, kmean, qt=1024):
    b, s, _ = q.shape
    nbp = kmean.shape[1]
    assert KEY_GROUP % qt == 0
    return pl.pallas_call(
        _moba_kernel,
        grid=(b, N_HEADS, s // qt),
        in_specs=[pl.BlockSpec((1, qt, HEAD_DIM), lambda bi, h, qi: (bi, qi, h)),
                  pl.BlockSpec((1, s, 2 * HEAD_DIM), lambda bi, h, qi: (bi, 0, h)),
                  pl.BlockSpec((1, s // KEY_GROUP, VT_ROWS, KEY_GROUP), lambda bi, h, qi: (h, bi, 0, 0)),
                  pl.BlockSpec((1, nbp, HEAD_DIM), lambda bi, h, qi: (bi, 0, h))],
        out_specs=pl.BlockSpec((1, qt, HEAD_DIM), lambda bi, h, qi: (bi, qi, h)),
        out_shape=jax.ShapeDtypeStruct((b, s, ATTN_WIDTH), BF16),
        scratch_shapes=[pltpu.VMEM((2, qt, HEAD_DIM + nbp), BF16),
                        pltpu.VMEM((2, KEY_GROUP, qt), F32),
                        pltpu.VMEM((2, KEY_GROUP, qt), BF16)],
        compiler_params=_cparams("parallel", "parallel", "arbitrary"),
        name="moba_attention",
    )(q, k_aug, vt, kmean)


def _merge_kernel(x_ref, a_ref, sb_ref, at_ref, gt_ref, wca_ref, wsg_ref, wat_ref, wo_ref,
                  gpost_ref, o_ref):
    d = D_MODEL
    merged = (gt_ref[:, 0:d].astype(F32) * _dot(a_ref[...], wca_ref[...])
              + gt_ref[:, d:2 * d].astype(F32) * _dot(sb_ref[...], wsg_ref[...])
              + gt_ref[:, 2 * d:3 * d].astype(F32) * _dot(at_ref[...], wat_ref[...]))
    h = _dot(merged.astype(BF16), wo_ref[...])
    o_ref[...] = x_ref[...] + _rms(h, gpost_ref[...])


def _merge(x, a_act, sb, attn, gates, conv_out, sgu_out, attn_out, w_o, g_post, tm=256):
    t, d = x.shape
    const = lambda i: (0, 0)
    wspec = lambda k: pl.BlockSpec((k, d), const, pipeline_mode=pl.Buffered(1))
    return pl.pallas_call(
        _merge_kernel,
        grid=(t // tm,),
        in_specs=[pl.BlockSpec((tm, d), lambda i: (i, 0)),
                  pl.BlockSpec((tm, CONV_CHANNELS), lambda i: (i, 0)),
                  pl.BlockSpec((tm, SGU_WIDTH), lambda i: (i, 0)),
                  pl.BlockSpec((tm, ATTN_WIDTH), lambda i: (i, 0)),
                  pl.BlockSpec((tm, 3 * d), lambda i: (i, 0)),
                  wspec(CONV_CHANNELS), wspec(SGU_WIDTH), wspec(ATTN_WIDTH), wspec(d),
                  pl.BlockSpec((1, d), const)],
        out_specs=pl.BlockSpec((tm, d), lambda i: (i, 0)),
        out_shape=jax.ShapeDtypeStruct((t, d), F32),
        compiler_params=_cparams("parallel"),
        name="merge",
    )(x, a_act, sb, attn, gates, conv_out, sgu_out, attn_out, w_o, g_post)


def _ffn_kernel(x_ref, xh_ref, gpre_ref, wa_ref, wb_ref, dww_ref, dwb_ref, wo_ref, gpost_ref,
                o_ref, hn_ref, acc_ref, *, tiles_per_seq):
    i = pl.program_id(0)
    j = pl.program_id(1)
    tm = x_ref.shape[0]

    @pl.when(j == 0)
    def _():
        halo = _rms(xh_ref[...], gpre_ref[...])
        hn_ref[0:FFN_HALO, :] = jnp.where(i % tiles_per_seq == 0, 0.0, halo).astype(BF16)
        hn_ref[FFN_HALO:, :] = _rms(x_ref[...], gpre_ref[...]).astype(BF16)
        acc_ref[...] = jnp.zeros_like(acc_ref)

    a = _dot(hn_ref[...], wa_ref[...])
    b = _dot(hn_ref[FFN_HALO:, :], wb_ref[...])
    conv = dwb_ref[...] + dww_ref[2:3, :] * a[FFN_HALO:]
    for k in (1, 2):
        conv = conv + dww_ref[2 - k:3 - k, :] * pltpu.roll(a, k, 0)[FFN_HALO:]
    act = (_gelu_tanh(conv) * b).astype(BF16)
    acc_ref[...] += _dot(act, wo_ref[...])

    @pl.when(j == pl.num_programs(1) - 1)
    def _():
        o_ref[...] = x_ref[...] + _rms(acc_ref[...], gpost_ref[...])


def _ffn(x, g_pre, ffn_in, dw_w, dw_b, ffn_out, g_post, seq, tm=512, tf=512):
    t, d = x.shape
    nf = D_FF // tf
    ratio = tm // FFN_HALO
    return pl.pallas_call(
        functools.partial(_ffn_kernel, tiles_per_seq=seq // tm),
        grid=(t // tm, nf),
        in_specs=[pl.BlockSpec((tm, d), lambda i, j: (i, 0)),
                  pl.BlockSpec((FFN_HALO, d), lambda i, j: (jnp.maximum(i * ratio - 1, 0), 0)),
                  pl.BlockSpec((1, d), lambda i, j: (0, 0)),
                  pl.BlockSpec((d, tf), lambda i, j: (0, j)),
                  pl.BlockSpec((d, tf), lambda i, j: (0, j + nf)),
                  pl.BlockSpec((FFN_CONV_KERNEL, tf), lambda i, j: (0, j)),
                  pl.BlockSpec((1, tf), lambda i, j: (0, j)),
                  pl.BlockSpec((tf, d), lambda i, j: (j, 0)),
                  pl.BlockSpec((1, d), lambda i, j: (0, 0))],
        out_specs=pl.BlockSpec((tm, d), lambda i, j: (i, 0)),
        out_shape=jax.ShapeDtypeStruct((t, d), F32),
        scratch_shapes=[pltpu.VMEM((tm + FFN_HALO, d), BF16), pltpu.VMEM((tm, d), F32)],
        compiler_params=_cparams("parallel", "arbitrary"),
        name="conv_ffn",
    )(x, x, g_pre, ffn_in, ffn_in, dw_w, dw_b, ffn_out, g_post)


def _ple_kernel(x_ref, p_ref, g_ref, wg_ref, wp_ref, o_ref):
    x = x_ref[...]
    gate = _sigmoid(_dot(_rms(x, g_ref[...]).astype(BF16), wg_ref[...]))
    o_ref[...] = x + gate * _dot(p_ref[...].astype(BF16), wp_ref[...])


def _ple(x, p, g, w_gate, w_proj, tm=512):
    t, d = x.shape
    const = lambda i: (0, 0)
    return pl.pallas_call(
        _ple_kernel,
        grid=(t // tm,),
        in_specs=[pl.BlockSpec((tm, d), lambda i: (i, 0)),
                  pl.BlockSpec((tm, PLE_DIM), lambda i: (i, 0)),
                  pl.BlockSpec((1, d), const),
                  pl.BlockSpec((d, d), const, pipeline_mode=pl.Buffered(1)),
                  pl.BlockSpec((PLE_DIM, d), const, pipeline_mode=pl.Buffered(1))],
        out_specs=pl.BlockSpec((tm, d), lambda i: (i, 0)),
        out_shape=jax.ShapeDtypeStruct((t, d), F32),
        compiler_params=_cparams("parallel"),
        name="ple",
    )(x, p, g, w_gate, w_proj)


def _rope_tables(seq):
    half = ROPE_DIM // 2
    inv = F32(ROPE_THETA) ** (-jnp.arange(0, ROPE_DIM, 2, dtype=F32) / ROPE_DIM)
    ang = jnp.arange(seq, dtype=jnp.int32).astype(F32)[:, None] * inv[None, :]
    cos, sin = jnp.cos(ang), jnp.sin(ang)
    zeros = jnp.zeros((seq, LANES - ROPE_DIM), F32)
    z16 = jnp.zeros((seq, half), F32)
    cos_t = jnp.concatenate([cos, cos, jnp.ones((seq, LANES - ROPE_DIM), F32)], axis=1)
    sin_lo = jnp.concatenate([-sin, z16, zeros], axis=1)
    sin_hi = jnp.concatenate([z16, sin, zeros], axis=1)
    return cos_t, sin_lo, sin_hi


def kernel(x, p, mix_norm_pre, mix_norm_post, w_in, conv_dw_w, conv_dw_b, conv_norm_g, conv_norm_b, conv_out, sgu_norm_g, sgu_norm_b, sgu_w, sgu_b, sgu_out, attn_out, w_o, ffn_norm_pre, ffn_norm_post, ffn_in, ffn_dw_w, ffn_dw_b, ffn_out, ple_norm, ple_gate, ple_proj):
    bsz, seq, d = x.shape
    depth = w_in.shape[0]
    assert d == D_MODEL and seq % KEY_GROUP == 0
    t = bsz * seq
    nb = seq // MOBA_BLOCK
    nbp = LANES * pl.cdiv(nb, LANES)
    assert nbp == LANES, "block-mask columns share one 256-deep contraction with the head dim"

    cos_t, sin_lo, sin_hi = _rope_tables(seq)
    row = lambda v: v.reshape(1, -1)
    c0 = 3 * ATTN_WIDTH
    c1 = c0 + CONV_CHANNELS
    c2 = c1 + CONV_CHANNELS
    c3 = c2 + SGU_WIDTH
    c4 = c3 + SGU_WIDTH

    xf = x.reshape(t, d)
    for i in range(depth):
        w = w_in[i].astype(BF16)
        xn = _prenorm(xf, row(mix_norm_pre[i]))
        q, k_aug, vt, kmean = _qkv(xn, w[:, :c0], cos_t, sin_lo, sin_hi, seq)
        a_act = _convmod(xn, w[:, c0:c1], w[:, c1:c2], conv_dw_w[i], row(conv_dw_b[i]),
                         row(conv_norm_g[i]), row(conv_norm_b[i]), seq)
        sb = _sgu(xn, w[:, c2:c3], w[:, c3:c4], row(sgu_norm_g[i]), row(sgu_norm_b[i]),
                  sgu_w[i], sgu_b[i].T)
        gates = _gates(xn, w[:, c4:])
        kmean = kmean.reshape(bsz, nb, ATTN_WIDTH)
        kmean = jnp.pad(kmean, ((0, 0), (0, nbp - nb), (0, 0))).astype(BF16)
        attn = _moba(q.reshape(bsz, seq, ATTN_WIDTH), k_aug.reshape(bsz, seq, 2 * ATTN_WIDTH), vt,
                     kmean).reshape(t, ATTN_WIDTH)
        xf = _merge(xf, a_act, sb, attn, gates, conv_out[i].astype(BF16), sgu_out[i].astype(BF16),
                    attn_out[i].astype(BF16), w_o[i].astype(BF16), row(mix_norm_post[i]))
        xf = _ffn(xf, row(ffn_norm_pre[i]), ffn_in[i].astype(BF16), ffn_dw_w[i], row(ffn_dw_b[i]),
                  ffn_out[i].astype(BF16), row(ffn_norm_post[i]), seq)
        xf = _ple(xf, p[i].reshape(t, PLE_DIM), row(ple_norm[i]), ple_gate[i].astype(BF16),
                  ple_proj[i].astype(BF16))
    return xf.reshape(bsz, seq, d)
```

```python
import functools

import jax
import jax.numpy as jnp
from jax import lax
from jax.experimental import pallas as pl
from jax.experimental.pallas import tpu as pltpu

F32 = jnp.float32
BF16 = jnp.bfloat16

D_MODEL = 2048
PLE_DIM = 256
N_HEADS = 8
HEAD_DIM = 128
ATTN_WIDTH = N_HEADS * HEAD_DIM
ROPE_DIM = HEAD_DIM // 4
ROPE_THETA = 500000.0
MOBA_BLOCK = 256
MOBA_TOPK = 3
CONV_CHANNELS = 512
CONV_KERNEL = 31
SGU_GROUPS = 4
SGU_GROUP_DIM = 128
SGU_WIDTH = SGU_GROUPS * SGU_GROUP_DIM
SGU_CHUNK = 128
D_FF = 5632
FFN_CONV_KERNEL = 3
N_BRANCHES = 3
EPS = 1e-6
CONV_COL = 3 * ATTN_WIDTH
SGU_COL = CONV_COL + 2 * CONV_CHANNELS
GATE_COL = SGU_COL + 2 * SGU_WIDTH

VMEM_LIMIT_BYTES = 56 * 1024 * 1024
LANES = 128
SUBLANES = 8
BF16_SUBLANES = 16
CONV_ROWS = 128
MASK_VALUE = -1e30

CONV_HALO = 32
FFN_HALO = BF16_SUBLANES
VT_ROWS = HEAD_DIM + BF16_SUBLANES
Q_SCALE = HEAD_DIM ** -0.5 * 1.4426950408889634
GROUP_BLOCKS = 4
KEY_GROUP = GROUP_BLOCKS * MOBA_BLOCK


def _cparams(*sem):
    return pltpu.CompilerParams(dimension_semantics=sem, vmem_limit_bytes=VMEM_LIMIT_BYTES)


def _rms(x, g):
    return x * lax.rsqrt(jnp.mean(x * x, axis=-1, keepdims=True) + EPS) * g


def _layer_norm(x, g, b):
    mu = jnp.mean(x, axis=-1, keepdims=True)
    xc = x - mu
    var = jnp.mean(xc * xc, axis=-1, keepdims=True)
    return xc * lax.rsqrt(var + EPS) * g + b


def _gelu_tanh(x):
    c = 0.7978845608028654
    return 0.5 * x * (1.0 + jnp.tanh(c * (x + 0.044715 * (x * x * x))))


def _sigmoid(x):
    return 1.0 / (1.0 + jnp.exp(-x))


def _dot(a, b):
    return jnp.dot(a, b, preferred_element_type=F32)


def _dot_nt(a, b):
    return lax.dot_general(a, b, (((1,), (1,)), ((), ())), preferred_element_type=F32)


def _prenorm_kernel(x_ref, g_ref, o_ref):
    o_ref[...] = _rms(x_ref[...], g_ref[...]).astype(BF16)


def _prenorm(x, g, tm=512):
    t, d = x.shape
    return pl.pallas_call(
        _prenorm_kernel,
        grid=(t // tm,),
        in_specs=[pl.BlockSpec((tm, d), lambda i: (i, 0)),
                  pl.BlockSpec((1, d), lambda i: (0, 0))],
        out_specs=pl.BlockSpec((tm, d), lambda i: (i, 0)),
        out_shape=jax.ShapeDtypeStruct((t, d), BF16),
        compiler_params=_cparams("parallel"),
        name="prenorm",
    )(x, g)


def _rope(xh, cos, sin_lo, sin_hi):
    half = ROPE_DIM // 2
    return (xh * cos + pltpu.roll(xh, LANES - half, 1) * sin_lo
            + pltpu.roll(xh, half, 1) * sin_hi)


def _qkv_kernel(xn_ref, w_ref, cos_ref, slo_ref, shi_ref, q_ref, k_ref, vt_ref, km_ref,
                *, tiles_per_seq):
    i = pl.program_id(0)
    j = pl.program_id(1)
    y = _dot(xn_ref[...], w_ref[...])
    tm = y.shape[0]
    nblk = tm // MOBA_BLOCK
    heads = [slice(h * HEAD_DIM, (h + 1) * HEAD_DIM) for h in range(N_HEADS)]

    @pl.when(j == 0)
    def _():
        cos, slo, shi = cos_ref[...], slo_ref[...], shi_ref[...]
        for sl in heads:
            q_ref[:, sl] = (_rope(y[:, sl], cos, slo, shi) * F32(Q_SCALE)).astype(BF16)

    @pl.when(j == 1)
    def _():
        cos, slo, shi = cos_ref[...], slo_ref[...], shi_ref[...]
        row = lax.broadcasted_iota(jnp.int32, (tm, LANES), 0)
        lane = lax.broadcasted_iota(jnp.int32, (tm, LANES), 1)
        first_blk = (i % tiles_per_seq) * nblk
        onehot = jnp.zeros((tm, LANES), F32)
        for c in range(nblk):
            in_c = jnp.logical_and(row >= c * MOBA_BLOCK, row < (c + 1) * MOBA_BLOCK)
            onehot = jnp.where(jnp.logical_and(in_c, lane == first_blk + c), 1.0, onehot)
        onehot = onehot.astype(BF16)
        for h, sl in enumerate(heads):
            r = _rope(y[:, sl], cos, slo, shi)
            k_ref[:, 2 * h * HEAD_DIM:(2 * h + 1) * HEAD_DIM] = r.astype(BF16)
            k_ref[:, (2 * h + 1) * HEAD_DIM:(2 * h + 2) * HEAD_DIM] = onehot
            for c in range(nblk):
                blk = r[c * MOBA_BLOCK:(c + 1) * MOBA_BLOCK]
                km_ref[0, c:c + 1, sl] = jnp.mean(blk, axis=0, keepdims=True)

    @pl.when(j == 2)
    def _():
        for h, sl in enumerate(heads):
            vt_ref[h, 0, 0:HEAD_DIM, :] = y[:, sl].T.astype(BF16)
            vt_ref[h, 0, HEAD_DIM:, :] = jnp.ones((VT_ROWS - HEAD_DIM, tm), BF16)


def _qkv(xn, w_in, cos, slo, shi, seq, tm=512):
    t, d = xn.shape
    tiles_per_seq = seq // tm
    nblk = tm // MOBA_BLOCK
    per_group = KEY_GROUP // tm
    assert per_group * tm == KEY_GROUP
    rope_spec =pl.BlockSpec((tm, LANES), lambda i, j: (i % tiles_per_seq, 0))
    return pl.pallas_call(
        functools.partial(_qkv_kernel, tiles_per_seq=tiles_per_seq),
        grid=(t // tm, 3),
        in_specs=[pl.BlockSpec((tm, d), lambda i, j: (i, 0)),
                  pl.BlockSpec((d, ATTN_WIDTH), lambda i, j: (0, j)),
                  rope_spec, rope_spec, rope_spec],
        out_specs=[pl.BlockSpec((tm, ATTN_WIDTH), lambda i, j: (i, 0)),
                   pl.BlockSpec((tm, 2 * ATTN_WIDTH), lambda i, j: (i, 0)),
                   pl.BlockSpec((N_HEADS, 1, VT_ROWS, tm), lambda i, j: (0, i // per_group, 0, i % per_group)),
                   pl.BlockSpec((1, nblk, ATTN_WIDTH), lambda i, j: (i, 0, 0))],
        out_shape=[jax.ShapeDtypeStruct((t, ATTN_WIDTH), BF16),
                   jax.ShapeDtypeStruct((t, 2 * ATTN_WIDTH), BF16),
                   jax.ShapeDtypeStruct((N_HEADS, t // KEY_GROUP, VT_ROWS, KEY_GROUP), BF16),
                   jax.ShapeDtypeStruct((t // tm, nblk, ATTN_WIDTH), F32)],
        compiler_params=_cparams("parallel", "arbitrary"),
        name="qkv_proj",
    )(xn, w_in, cos, slo, shi)


def _convmod_kernel(xn_ref, xh_ref, wa_ref, wg_ref, dww_ref, dwb_ref, lng_ref, lnb_ref,
                    o_ref, a_ref, sh_ref, *, tiles_per_seq):
    i = pl.program_id(0)
    tm = xn_ref.shape[0]

    def glu(xn):
        return _dot(xn, wa_ref[...]) * _sigmoid(_dot(xn, wg_ref[...]))

    halo = glu(xh_ref[...])
    a_ref[0:CONV_HALO, :] = jnp.where(i % tiles_per_seq == 0, 0.0, halo)
    a_ref[CONV_HALO:, :] = glu(xn_ref[...])

    rb = CONV_ROWS
    first = CONV_HALO - (CONV_KERNEL - 1)
    for r in range(tm // rb):
        cols = []
        for c in range(CONV_CHANNELS // LANES):
            cs = slice(c * LANES, (c + 1) * LANES)
            acc = dwb_ref[:, cs]
            for j in range(SUBLANES):
                rows = rb if j == 0 else rb + SUBLANES
                part = None
                for k in range(CONV_KERNEL):
                    if (first + k) % SUBLANES == j:
                        term = dww_ref[k:k + 1, cs] * a_ref[pl.ds(r * rb + first + k - j, rows), cs]
                        part = term if part is None else part + term
                if j > 0:
                    sh_ref[j] = part
                    part = sh_ref[j, pl.ds(j, rb), :]
                acc = acc + part
            cols.append(acc)
        y = jnp.concatenate(cols, axis=1)
        y = _layer_norm(y, lng_ref[...], lnb_ref[...])
        o_ref[r * rb:(r + 1) * rb, :] = (y * _sigmoid(y)).astype(BF16)


def _convmod(xn, w_in, dw_w, dw_b, ln_g, ln_b, seq, tm=512):
    t, d = xn.shape
    c = CONV_CHANNELS
    col = CONV_COL // c
    ratio = tm // CONV_HALO
    const = lambda i: (0, 0)
    return pl.pallas_call(
        functools.partial(_convmod_kernel, tiles_per_seq=seq // tm),
        grid=(t // tm,),
        in_specs=[pl.BlockSpec((tm, d), lambda i: (i, 0)),
                  pl.BlockSpec((CONV_HALO, d), lambda i: (jnp.maximum(i * ratio - 1, 0), 0)),
                  pl.BlockSpec((d, c), lambda i: (0, col)), pl.BlockSpec((d, c), lambda i: (0, col + 1)),
                  pl.BlockSpec((CONV_KERNEL, c), const), pl.BlockSpec((1, c), const),
                  pl.BlockSpec((1, c), const), pl.BlockSpec((1, c), const)],
        out_specs=pl.BlockSpec((tm, c), lambda i: (i, 0)),
        out_shape=jax.ShapeDtypeStruct((t, c), BF16),
        scratch_shapes=[pltpu.VMEM((tm + CONV_HALO, c), F32),
                        pltpu.VMEM((SUBLANES, CONV_ROWS + SUBLANES, LANES), F32)],
        compiler_params=_cparams("parallel"),
        name="conv_module",
    )(xn, xn, w_in, w_in, dw_w, dw_b, ln_g, ln_b)


def _sgu_kernel(xn_ref, wu_ref, wv_ref, lng_ref, lnb_ref, sw_ref, sbt_ref, o_ref):
    tm = xn_ref.shape[0]
    xn = xn_ref[...]
    u = _gelu_tanh(_dot(xn, wu_ref[...]))
    v = _layer_norm(_gelu_tanh(_dot(xn, wv_ref[...])), lng_ref[...], lnb_ref[...]).astype(BF16)
    row = lax.broadcasted_iota(jnp.int32, (SGU_CHUNK, SGU_CHUNK), 0)
    col = lax.broadcasted_iota(jnp.int32, (SGU_CHUNK, SGU_CHUNK), 1)
    for g in range(SGU_GROUPS):
        gs = slice(g * SGU_GROUP_DIM, (g + 1) * SGU_GROUP_DIM)
        w = jnp.where(col <= row, sw_ref[g], 0.0).astype(BF16)
        bias = sbt_ref[:, g:g + 1]
        for n in range(tm // SGU_CHUNK):
            rs = slice(n * SGU_CHUNK, (n + 1) * SGU_CHUNK)
            mixed = _dot(w, v[rs, gs]) + bias
            o_ref[rs, gs] = (u[rs, gs] * mixed).astype(BF16)


def _sgu(xn, w_in, ln_g, ln_b, sgu_w, sgu_bt, tm=512):
    t, d = xn.shape
    c = SGU_WIDTH
    col = SGU_COL // c
    const = lambda i: (0, 0)
    return pl.pallas_call(
        _sgu_kernel,
        grid=(t // tm,),
        in_specs=[pl.BlockSpec((tm, d), lambda i: (i, 0)),
                  pl.BlockSpec((d, c), lambda i: (0, col)), pl.BlockSpec((d, c), lambda i: (0, col + 1)),
                  pl.BlockSpec((1, c), const), pl.BlockSpec((1, c), const),
                  pl.BlockSpec((SGU_GROUPS, SGU_CHUNK, SGU_CHUNK), lambda i: (0, 0, 0)),
                  pl.BlockSpec((SGU_CHUNK, SGU_GROUPS), const)],
        out_specs=pl.BlockSpec((tm, c), lambda i: (i, 0)),
        out_shape=jax.ShapeDtypeStruct((t, c), BF16),
        compiler_params=_cparams("parallel"),
        name="sgu",
    )(xn, w_in, w_in, ln_g, ln_b, sgu_w, sgu_bt)


def _gates_kernel(xn_ref, w_ref, o_ref):
    o_ref[...] = _sigmoid(_dot(xn_ref[...], w_ref[...])).astype(BF16)


def _gates(xn, w_in, tm=512, tn=1024):
    t, d = xn.shape
    n = N_BRANCHES * D_MODEL
    col = GATE_COL // tn
    return pl.pallas_call(
        _gates_kernel,
        grid=(t // tm, n // tn),
        in_specs=[pl.BlockSpec((tm, d), lambda i, j: (i, 0)),
                  pl.BlockSpec((d, tn), lambda i, j: (0, col + j))],
        out_specs=pl.BlockSpec((tm, tn), lambda i, j: (i, j)),
        out_shape=jax.ShapeDtypeStruct((t, n), BF16),
        compiler_params=_cparams("parallel", "arbitrary"),
        name="gates",
    )(xn, w_in)


def _moba_kernel(q_ref, k_ref, vt_ref, km_ref, o_ref, qa_ref, s_ref, p_ref):
    qi = pl.program_id(2)
    qt = q_ref.shape[1]
    q_blocks = qt // MOBA_BLOCK
    q = q_ref[0]
    nbp = km_ref.shape[1]
    nb_rows = SUBLANES * pl.cdiv(k_ref.shape[1] // MOBA_BLOCK, SUBLANES)

    gate = _dot_nt(km_ref[0, 0:nb_rows], q)
    blk_i = lax.broadcasted_iota(jnp.int32, gate.shape, 0)
    qry_i = lax.broadcasted_iota(jnp.int32, gate.shape, 1)
    own = qi * q_blocks + sum((qry_i >= r * MOBA_BLOCK).astype(jnp.int32) for r in range(1, q_blocks))
    blkf = blk_i.astype(F32)
    past = blk_i < own
    g = jnp.where(past, gate, -jnp.inf)
    sel = jnp.zeros(gate.shape, jnp.bool_)
    for _ in range(MOBA_TOPK):
        top = jnp.max(g, axis=0, keepdims=True)
        first = jnp.min(jnp.where(g == top, blkf, F32(nbp)), axis=0, keepdims=True)
        pick = blkf == first
        sel = jnp.logical_or(sel, pick)
        g = jnp.where(pick, -jnp.inf, g)
    allowed = jnp.logical_or(jnp.logical_and(sel, past), blk_i == own)
    bias = jnp.where(allowed, 0.0, MASK_VALUE)
    if nbp > nb_rows:
        bias = jnp.concatenate([bias, jnp.full((nbp - nb_rows, qt), MASK_VALUE, F32)], axis=0)
    bias = bias.T.astype(BF16)
    qa_ref[0] = jnp.concatenate([q, bias], axis=1)
    qa_ref[1] = jnp.concatenate([q, jnp.full(bias.shape, MASK_VALUE, BF16)], axis=1)

    own_g = (qi * q_blocks) // GROUP_BLOCKS
    pad_t = jnp.where(own_g % 2 == 0, own_g, -1)
    n_all = own_g + (1 - own_g % 2) + 1

    def qk_stage(t, slot):
        g = jnp.minimum(t, own_g)
        st = pl.multiple_of(g * KEY_GROUP, KEY_GROUP)
        which = (t == pad_t).astype(jnp.int32)
        s = _dot_nt(k_ref[0, pl.ds(st, KEY_GROUP), :], qa_ref[which])
        s_ref[slot] = s
        return jnp.max(s, axis=0, keepdims=True)

    def softmax_stage(slot, m, s_max=None):
        s = s_ref[slot]
        if s_max is None:
            key_i = lax.broadcasted_iota(jnp.int32, s.shape, 0)
            qry_i = (lax.broadcasted_iota(jnp.int32, s.shape, 1)
                     + ((qi * q_blocks) % GROUP_BLOCKS) * MOBA_BLOCK)
            s = jnp.where(key_i <= qry_i, s, MASK_VALUE)
            s_max = jnp.max(s, axis=0, keepdims=True)
        m_new = jnp.maximum(m, s_max)
        p_ref[slot] = jnp.exp2(s - m_new).astype(BF16)
        return m_new, jnp.exp2(m - m_new)

    def pv_stage(t, slot, acc, alpha):
        return alpha * acc + _dot(vt_ref[0, jnp.minimum(t, own_g)], p_ref[slot])

    m = jnp.full((1, qt), MASK_VALUE, F32)
    acc = jnp.zeros((VT_ROWS, qt), F32)
    max0 = qk_stage(0, 0)
    max1 = qk_stage(1, 1)
    m, a0 = softmax_stage(0, m, max0)

    def pair(u, carry):
        m, acc, a0, max1 = carry
        t = 2 + 2 * u
        max0 = qk_stage(t, 0)
        m, a1 = softmax_stage(1, m, max1)
        acc = pv_stage(t - 2, 0, acc, a0)
        max1 = qk_stage(t + 1, 1)
        m, a0 = softmax_stage(0, m, max0)
        acc = pv_stage(t - 1, 1, acc, a1)
        return m, acc, a0, max1

    m, acc, a0, _ = lax.fori_loop(0, (n_all - 2) // 2, pair, (m, acc, a0, max1))
    m, a1 = softmax_stage(1, m)
    acc = pv_stage(n_all - 2, 0, acc, a0)
    acc = pv_stage(n_all - 1, 1, acc, a1)
    out_t = acc[0:HEAD_DIM] / acc[HEAD_DIM:HEAD_DIM + 1]
    o_ref[0] = out_t.T.astype(BF16)


def _moba(q, k_aug, vt, kmean, qt=1024):
    b, s, _ = q.shape
    nbp = kmean.shape[1]
    assert KEY_GROUP % qt == 0
    return pl.pallas_call(
        _moba_kernel,
        grid=(b, N_HEADS, s // qt),
        in_specs=[pl.BlockSpec((1, qt, HEAD_DIM), lambda bi, h, qi: (bi, qi, h)),
                  pl.BlockSpec((1, s, 2 * HEAD_DIM), lambda bi, h, qi: (bi, 0, h)),
                  pl.BlockSpec((1, s // KEY_GROUP, VT_ROWS, KEY_GROUP), lambda bi, h, qi: (h, bi, 0, 0)),
                  pl.BlockSpec((1, nbp, HEAD_DIM), lambda bi, h, qi: (bi, 0, h))],
        out_specs=pl.BlockSpec((1, qt, HEAD_DIM), lambda bi, h, qi: (bi, qi, h)),
        out_shape=jax.ShapeDtypeStruct((b, s, ATTN_WIDTH), BF16),
        scratch_shapes=[pltpu.VMEM((2, qt, HEAD_DIM + nbp), BF16),
                        pltpu.VMEM((2, KEY_GROUP, qt), F32),
                        pltpu.VMEM((2, KEY_GROUP, qt), BF16)],
        compiler_params=_cparams("parallel", "parallel", "arbitrary"),
        name="moba_attention",
    )(q, k_aug, vt, kmean)


def _merge_kernel(x_ref, a_ref, sb_ref, at_ref, gt_ref, wca_ref, wsg_ref, wat_ref, wo_ref,
                  gpost_ref, o_ref):
    d = D_MODEL
    merged = (gt_ref[:, 0:d].astype(F32) * _dot(a_ref[...], wca_ref[...])
              + gt_ref[:, d:2 * d].astype(F32) * _dot(sb_ref[...], wsg_ref[...])
              + gt_ref[:, 2 * d:3 * d].astype(F32) * _dot(at_ref[...], wat_ref[...]))
    h = _dot(merged.astype(BF16), wo_ref[...])
    o_ref[...] = x_ref[...] + _rms(h, gpost_ref[...])


def _merge(x, a_act, sb, attn, gates, conv_out, sgu_out, attn_out, w_o, g_post, tm=256):
    t, d = x.shape
    const = lambda i: (0, 0)
    wspec = lambda k: pl.BlockSpec((k, d), const, pipeline_mode=pl.Buffered(1))
    return pl.pallas_call(
        _merge_kernel,
        grid=(t // tm,),
        in_specs=[pl.BlockSpec((tm, d), lambda i: (i, 0)),
                  pl.BlockSpec((tm, CONV_CHANNELS), lambda i: (i, 0)),
                  pl.BlockSpec((tm, SGU_WIDTH), lambda i: (i, 0)),
                  pl.BlockSpec((tm, ATTN_WIDTH), lambda i: (i, 0)),
                  pl.BlockSpec((tm, 3 * d), lambda i: (i, 0)),
                  wspec(CONV_CHANNELS), wspec(SGU_WIDTH), wspec(ATTN_WIDTH), wspec(d),
                  pl.BlockSpec((1, d), const)],
        out_specs=pl.BlockSpec((tm, d), lambda i: (i, 0)),
        out_shape=jax.ShapeDtypeStruct((t, d), F32),
        compiler_params=_cparams("parallel"),
        name="merge",
    )(x, a_act, sb, attn, gates, conv_out, sgu_out, attn_out, w_o, g_post)


def _ffn_kernel(x_ref, xh_ref, gpre_ref, wa_ref, wb_ref, dww_ref, dwb_ref, wo_ref, gpost_ref,
                o_ref, hn_ref, acc_ref, *, tiles_per_seq):
    i = pl.program_id(0)
    j = pl.program_id(1)
    tm = x_ref.shape[0]

    @pl.when(j == 0)
    def _():
        halo = _rms(xh_ref[...], gpre_ref[...])
        hn_ref[0:FFN_HALO, :] = jnp.where(i % tiles_per_seq == 0, 0.0, halo).astype(BF16)
        hn_ref[FFN_HALO:, :] = _rms(x_ref[...], gpre_ref[...]).astype(BF16)
        acc_ref[...] = jnp.zeros_like(acc_ref)

    a = _dot(hn_ref[...], wa_ref[...])
    b = _dot(hn_ref[FFN_HALO:, :], wb_ref[...])
    conv = dwb_ref[...] + dww_ref[2:3, :] * a[FFN_HALO:]
    for k in (1, 2):
        conv = conv + dww_ref[2 - k:3 - k, :] * pltpu.roll(a, k, 0)[FFN_HALO:]
    act = (_gelu_tanh(conv) * b).astype(BF16)
    acc_ref[...] += _dot(act, wo_ref[...])

    @pl.when(j == pl.num_programs(1) - 1)
    def _():
        o_ref[...] = x_ref[...] + _rms(acc_ref[...], gpost_ref[...])


def _ffn(x, g_pre, ffn_in, dw_w, dw_b, ffn_out, g_post, seq, tm=512, tf=512):
    t, d = x.shape
    nf = D_FF // tf
    ratio = tm // FFN_HALO
    return pl.pallas_call(
        functools.partial(_ffn_kernel, tiles_per_seq=seq // tm),
        grid=(t // tm, nf),
        in_specs=[pl.BlockSpec((tm, d), lambda i, j: (i, 0)),
                  pl.BlockSpec((FFN_HALO, d), lambda i, j: (jnp.maximum(i * ratio - 1, 0), 0)),
                  pl.BlockSpec((1, d), lambda i, j: (0, 0)),
                  pl.BlockSpec((d, tf), lambda i, j: (0, j)),
                  pl.BlockSpec((d, tf), lambda i, j: (0, j + nf)),
                  pl.BlockSpec((FFN_CONV_KERNEL, tf), lambda i, j: (0, j)),
                  pl.BlockSpec((1, tf), lambda i, j: (0, j)),
                  pl.BlockSpec((tf, d), lambda i, j: (j, 0)),
                  pl.BlockSpec((1, d), lambda i, j: (0, 0))],
        out_specs=pl.BlockSpec((tm, d), lambda i, j: (i, 0)),
        out_shape=jax.ShapeDtypeStruct((t, d), F32),
        scratch_shapes=[pltpu.VMEM((tm + FFN_HALO, d), BF16), pltpu.VMEM((tm, d), F32)],
        compiler_params=_cparams("parallel", "arbitrary"),
        name="conv_ffn",
    )(x, x, g_pre, ffn_in, ffn_in, dw_w, dw_b, ffn_out, g_post)


def _ple_kernel(x_ref, p_ref, g_ref, wg_ref, wp_ref, o_ref):
    x = x_ref[...]
    gate = _sigmoid(_dot(_rms(x, g_ref[...]).astype(BF16), wg_ref[...]))
    o_ref[...] = x + gate * _dot(p_ref[...].astype(BF16), wp_ref[...])


def _ple(x, p, g, w_gate, w_proj, tm=512):
    t, d = x.shape
    const = lambda i: (0, 0)
    return pl.pallas_call(
        _ple_kernel,
        grid=(t // tm,),
        in_specs=[pl.BlockSpec((tm, d), lambda i: (i, 0)),
                  pl.BlockSpec((tm, PLE_DIM), lambda i: (i, 0)),
                  pl.BlockSpec((1, d), const),
                  pl.BlockSpec((d, d), const, pipeline_mode=pl.Buffered(1)),
                  pl.BlockSpec((PLE_DIM, d), const, pipeline_mode=pl.Buffered(1))],
        out_specs=pl.BlockSpec((tm, d), lambda i: (i, 0)),
        out_shape=jax.ShapeDtypeStruct((t, d), F32),
        compiler_params=_cparams("parallel"),
        name="ple",
    )(x, p, g, w_gate, w_proj)


def _rope_tables(seq):
    half = ROPE_DIM // 2
    inv = F32(ROPE_THETA) ** (-jnp.arange(0, ROPE_DIM, 2, dtype=F32) / ROPE_DIM)
    ang = jnp.arange(seq, dtype=jnp.int32).astype(F32)[:, None] * inv[None, :]
    cos, sin = jnp.cos(ang), jnp.sin(ang)
    zeros = jnp.zeros((seq, LANES - ROPE_DIM), F32)
    z16 = jnp.zeros((seq, half), F32)
    cos_t = jnp.concatenate([cos, cos, jnp.ones((seq, LANES - ROPE_DIM), F32)], axis=1)
    sin_lo = jnp.concatenate([-sin, z16, zeros], axis=1)
    sin_hi = jnp.concatenate([z16, sin, zeros], axis=1)
    return cos_t, sin_lo, sin_hi


def kernel(x, p, mix_norm_pre, mix_norm_post, w_in, conv_dw_w, conv_dw_b, conv_norm_g, conv_norm_b, conv_out, sgu_norm_g, sgu_norm_b, sgu_w, sgu_b, sgu_out, attn_out, w_o, ffn_norm_pre, ffn_norm_post, ffn_in, ffn_dw_w, ffn_dw_b, ffn_out, ple_norm, ple_gate, ple_proj):
    bsz, seq, d = x.shape
    depth = w_in.shape[0]
    assert d == D_MODEL and seq % KEY_GROUP == 0
    t = bsz * seq
    nb = seq // MOBA_BLOCK
    nbp = LANES * pl.cdiv(nb, LANES)
    assert nbp == LANES, "block-mask columns share one 256-deep contraction with the head dim"

    cos_t, sin_lo, sin_hi = _rope_tables(seq)
    row = lambda v: v.reshape(1, -1)

    xf = x.reshape(t, d)
    for i in range(depth):
        w = w_in[i].astype(BF16)
        xn = _prenorm(xf, row(mix_norm_pre[i]))
        q, k_aug, vt, kmean = _qkv(xn, w, cos_t, sin_lo, sin_hi, seq)
        a_act = _convmod(xn, w, conv_dw_w[i], row(conv_dw_b[i]),
                         row(conv_norm_g[i]), row(conv_norm_b[i]), seq)
        sb = _sgu(xn, w, row(sgu_norm_g[i]), row(sgu_norm_b[i]), sgu_w[i], sgu_b[i].T)
        gates = _gates(xn, w)
        kmean = kmean.reshape(bsz, nb, ATTN_WIDTH)
        kmean = jnp.pad(kmean, ((0, 0), (0, nbp - nb), (0, 0))).astype(BF16)
        attn = _moba(q.reshape(bsz, seq, ATTN_WIDTH), k_aug.reshape(bsz, seq, 2 * ATTN_WIDTH), vt,
                     kmean).reshape(t, ATTN_WIDTH)
        xf = _merge(xf, a_act, sb, attn, gates, conv_out[i].astype(BF16), sgu_out[i].astype(BF16),
                    attn_out[i].astype(BF16), w_o[i].astype(BF16), row(mix_norm_post[i]))
        xf = _ffn(xf, row(ffn_norm_pre[i]), ffn_in[i].astype(BF16), ffn_dw_w[i], row(ffn_dw_b[i]),
                  ffn_out[i].astype(BF16), row(ffn_norm_post[i]), seq)
        xf = _ple(xf, p[i].reshape(t, PLE_DIM), row(ple_norm[i]), ple_gate[i].astype(BF16),
                  ple_proj[i].astype(BF16))
    return xf.reshape(bsz, seq, d)
```

```python
import functools

import jax
import jax.numpy as jnp
from jax import lax
from jax.experimental import pallas as pl
from jax.experimental.pallas import tpu as pltpu

F32 = jnp.float32
BF16 = jnp.bfloat16

D_MODEL = 2048
PLE_DIM = 256
N_HEADS = 8
HEAD_DIM = 128
ATTN_WIDTH = N_HEADS * HEAD_DIM
ROPE_DIM = HEAD_DIM // 4
ROPE_THETA = 500000.0
MOBA_BLOCK = 256
MOBA_TOPK = 3
CONV_CHANNELS = 512
CONV_KERNEL = 31
SGU_GROUPS = 4
SGU_GROUP_DIM = 128
SGU_WIDTH = SGU_GROUPS * SGU_GROUP_DIM
SGU_CHUNK = 128
D_FF = 5632
FFN_CONV_KERNEL = 3
N_BRANCHES = 3
EPS = 1e-6
CONV_COL = 3 * ATTN_WIDTH
SGU_COL = CONV_COL + 2 * CONV_CHANNELS
GATE_COL = SGU_COL + 2 * SGU_WIDTH

VMEM_LIMIT_BYTES = 56 * 1024 * 1024
LANES = 128
SUBLANES = 8
BF16_SUBLANES = 16
CONV_ROWS = 128
MASK_VALUE = -1e30

CONV_HALO = 32
FFN_HALO = BF16_SUBLANES
VT_ROWS = HEAD_DIM + BF16_SUBLANES
Q_SCALE = HEAD_DIM ** -0.5 * 1.4426950408889634
GROUP_BLOCKS = 4
KEY_GROUP = GROUP_BLOCKS * MOBA_BLOCK


def _cparams(*sem):
    return pltpu.CompilerParams(dimension_semantics=sem, vmem_limit_bytes=VMEM_LIMIT_BYTES)


def _rms(x, g):
    return x * lax.rsqrt(jnp.mean(x * x, axis=-1, keepdims=True) + EPS) * g


def _layer_norm(x, g, b):
    mu = jnp.mean(x, axis=-1, keepdims=True)
    xc = x - mu
    var = jnp.mean(xc * xc, axis=-1, keepdims=True)
    return xc * lax.rsqrt(var + EPS) * g + b


def _gelu_tanh(x):
    c = 0.7978845608028654
    return 0.5 * x * (1.0 + jnp.tanh(c * (x + 0.044715 * (x * x * x))))


def _sigmoid(x):
    return 1.0 / (1.0 + jnp.exp(-x))


def _dot(a, b):
    return jnp.dot(a, b, preferred_element_type=F32)


def _dot_nt(a, b):
    return lax.dot_general(a, b, (((1,), (1,)), ((), ())), preferred_element_type=F32)


def _prenorm_kernel(x_ref, g_ref, o_ref):
    o_ref[...] = _rms(x_ref[...], g_ref[...]).astype(BF16)


def _prenorm(x, g, tm=512):
    t, d = x.shape
    return pl.pallas_call(
        _prenorm_kernel,
        grid=(t // tm,),
        in_specs=[pl.BlockSpec((tm, d), lambda i: (i, 0)),
                  pl.BlockSpec((1, d), lambda i: (0, 0))],
        out_specs=pl.BlockSpec((tm, d), lambda i: (i, 0)),
        out_shape=jax.ShapeDtypeStruct((t, d), BF16),
        compiler_params=_cparams("parallel"),
        name="prenorm",
    )(x, g)


def _rope(xh, cos, sin_lo, sin_hi):
    half = ROPE_DIM // 2
    return (xh * cos + pltpu.roll(xh, LANES - half, 1) * sin_lo
            + pltpu.roll(xh, half, 1) * sin_hi)


def _qkv_kernel(xn_ref, w_ref, cos_ref, slo_ref, shi_ref, q_ref, k_ref, vt_ref, km_ref,
                *, tiles_per_seq):
    i = pl.program_id(0)
    j = pl.program_id(1)
    y = _dot(xn_ref[...], w_ref[...])
    tm = y.shape[0]
    nblk = tm // MOBA_BLOCK
    heads = [slice(h * HEAD_DIM, (h + 1) * HEAD_DIM) for h in range(N_HEADS)]

    @pl.when(j == 0)
    def _():
        cos, slo, shi = cos_ref[...], slo_ref[...], shi_ref[...]
        for sl in heads:
            q_ref[:, sl] = (_rope(y[:, sl], cos, slo, shi) * F32(Q_SCALE)).astype(BF16)

    @pl.when(j == 1)
    def _():
        cos, slo, shi = cos_ref[...], slo_ref[...], shi_ref[...]
        row = lax.broadcasted_iota(jnp.int32, (tm, LANES), 0)
        lane = lax.broadcasted_iota(jnp.int32, (tm, LANES), 1)
        first_blk = (i % tiles_per_seq) * nblk
        onehot = jnp.zeros((tm, LANES), F32)
        for c in range(nblk):
            in_c = jnp.logical_and(row >= c * MOBA_BLOCK, row < (c + 1) * MOBA_BLOCK)
            onehot = jnp.where(jnp.logical_and(in_c, lane == first_blk + c), 1.0, onehot)
        onehot = onehot.astype(BF16)
        for h, sl in enumerate(heads):
            r = _rope(y[:, sl], cos, slo, shi)
            k_ref[:, 2 * h * HEAD_DIM:(2 * h + 1) * HEAD_DIM] = r.astype(BF16)
            k_ref[:, (2 * h + 1) * HEAD_DIM:(2 * h + 2) * HEAD_DIM] = onehot
            for c in range(nblk):
                blk = r[c * MOBA_BLOCK:(c + 1) * MOBA_BLOCK]
                km_ref[0, c:c + 1, sl] = jnp.mean(blk, axis=0, keepdims=True)

    @pl.when(j == 2)
    def _():
        for h, sl in enumerate(heads):
            vt_ref[h, 0, 0:HEAD_DIM, :] = y[:, sl].T.astype(BF16)
            vt_ref[h, 0, HEAD_DIM:, :] = jnp.ones((VT_ROWS - HEAD_DIM, tm), BF16)


def _qkv(xn, w_in, cos, slo, shi, seq, tm=1024):
    t, d = xn.shape
    tiles_per_seq = seq // tm
    nblk = tm // MOBA_BLOCK
    per_group = KEY_GROUP // tm
    assert per_group * tm == KEY_GROUP
    rope_spec =pl.BlockSpec((tm, LANES), lambda i, j: (i % tiles_per_seq, 0))
    return pl.pallas_call(
        functools.partial(_qkv_kernel, tiles_per_seq=tiles_per_seq),
        grid=(t // tm, 3),
        in_specs=[pl.BlockSpec((tm, d), lambda i, j: (i, 0)),
                  pl.BlockSpec((d, ATTN_WIDTH), lambda i, j: (0, j)),
                  rope_spec, rope_spec, rope_spec],
        out_specs=[pl.BlockSpec((tm, ATTN_WIDTH), lambda i, j: (i, 0)),
                   pl.BlockSpec((tm, 2 * ATTN_WIDTH), lambda i, j: (i, 0)),
                   pl.BlockSpec((N_HEADS, 1, VT_ROWS, tm), lambda i, j: (0, i // per_group, 0, i % per_group)),
                   pl.BlockSpec((1, nblk, ATTN_WIDTH), lambda i, j: (i, 0, 0))],
        out_shape=[jax.ShapeDtypeStruct((t, ATTN_WIDTH), BF16),
                   jax.ShapeDtypeStruct((t, 2 * ATTN_WIDTH), BF16),
                   jax.ShapeDtypeStruct((N_HEADS, t // KEY_GROUP, VT_ROWS, KEY_GROUP), BF16),
                   jax.ShapeDtypeStruct((t // tm, nblk, ATTN_WIDTH), F32)],
        compiler_params=_cparams("parallel", "arbitrary"),
        name="qkv_proj",
    )(xn, w_in, cos, slo, shi)


def _convmod_kernel(xn_ref, xh_ref, wa_ref, wg_ref, dww_ref, dwb_ref, lng_ref, lnb_ref,
                    o_ref, a_ref, sh_ref, *, tiles_per_seq):
    i = pl.program_id(0)
    tm = xn_ref.shape[0]

    def glu(xn):
        return _dot(xn, wa_ref[...]) * _sigmoid(_dot(xn, wg_ref[...]))

    halo = glu(xh_ref[...])
    a_ref[0:CONV_HALO, :] = jnp.where(i % tiles_per_seq == 0, 0.0, halo)
    a_ref[CONV_HALO:, :] = glu(xn_ref[...])

    rb = CONV_ROWS
    first = CONV_HALO - (CONV_KERNEL - 1)
    for r in range(tm // rb):
        cols = []
        for c in range(CONV_CHANNELS // LANES):
            cs = slice(c * LANES, (c + 1) * LANES)
            acc = dwb_ref[:, cs]
            for j in range(SUBLANES):
                rows = rb if j == 0 else rb + SUBLANES
                part = None
                for k in range(CONV_KERNEL):
                    if (first + k) % SUBLANES == j:
                        term = dww_ref[k:k + 1, cs] * a_ref[pl.ds(r * rb + first + k - j, rows), cs]
                        part = term if part is None else part + term
                if j > 0:
                    sh_ref[j] = part
                    part = sh_ref[j, pl.ds(j, rb), :]
                acc = acc + part
            cols.append(acc)
        y = jnp.concatenate(cols, axis=1)
        y = _layer_norm(y, lng_ref[...], lnb_ref[...])
        o_ref[r * rb:(r + 1) * rb, :] = (y * _sigmoid(y)).astype(BF16)


def _convmod(xn, w_in, dw_w, dw_b, ln_g, ln_b, seq, tm=512):
    t, d = xn.shape
    c = CONV_CHANNELS
    col = CONV_COL // c
    ratio = tm // CONV_HALO
    const = lambda i: (0, 0)
    return pl.pallas_call(
        functools.partial(_convmod_kernel, tiles_per_seq=seq // tm),
        grid=(t // tm,),
        in_specs=[pl.BlockSpec((tm, d), lambda i: (i, 0)),
                  pl.BlockSpec((CONV_HALO, d), lambda i: (jnp.maximum(i * ratio - 1, 0), 0)),
                  pl.BlockSpec((d, c), lambda i: (0, col)), pl.BlockSpec((d, c), lambda i: (0, col + 1)),
                  pl.BlockSpec((CONV_KERNEL, c), const), pl.BlockSpec((1, c), const),
                  pl.BlockSpec((1, c), const), pl.BlockSpec((1, c), const)],
        out_specs=pl.BlockSpec((tm, c), lambda i: (i, 0)),
        out_shape=jax.ShapeDtypeStruct((t, c), BF16),
        scratch_shapes=[pltpu.VMEM((tm + CONV_HALO, c), F32),
                        pltpu.VMEM((SUBLANES, CONV_ROWS + SUBLANES, LANES), F32)],
        compiler_params=_cparams("parallel"),
        name="conv_module",
    )(xn, xn, w_in, w_in, dw_w, dw_b, ln_g, ln_b)


def _sgu_kernel(xn_ref, wu_ref, wv_ref, lng_ref, lnb_ref, sw_ref, sbt_ref, o_ref):
    tm = xn_ref.shape[0]
    xn = xn_ref[...]
    u = _gelu_tanh(_dot(xn, wu_ref[...]))
    v = _layer_norm(_gelu_tanh(_dot(xn, wv_ref[...])), lng_ref[...], lnb_ref[...]).astype(BF16)
    row = lax.broadcasted_iota(jnp.int32, (SGU_CHUNK, SGU_CHUNK), 0)
    col = lax.broadcasted_iota(jnp.int32, (SGU_CHUNK, SGU_CHUNK), 1)
    for g in range(SGU_GROUPS):
        gs = slice(g * SGU_GROUP_DIM, (g + 1) * SGU_GROUP_DIM)
        w = jnp.where(col <= row, sw_ref[g], 0.0).astype(BF16)
        bias = sbt_ref[:, g:g + 1]
        for n in range(tm // SGU_CHUNK):
            rs = slice(n * SGU_CHUNK, (n + 1) * SGU_CHUNK)
            mixed = _dot(w, v[rs, gs]) + bias
            o_ref[rs, gs] = (u[rs, gs] * mixed).astype(BF16)


def _sgu(xn, w_in, ln_g, ln_b, sgu_w, sgu_bt, tm=512):
    t, d = xn.shape
    c = SGU_WIDTH
    col = SGU_COL // c
    const = lambda i: (0, 0)
    return pl.pallas_call(
        _sgu_kernel,
        grid=(t // tm,),
        in_specs=[pl.BlockSpec((tm, d), lambda i: (i, 0)),
                  pl.BlockSpec((d, c), lambda i: (0, col)), pl.BlockSpec((d, c), lambda i: (0, col + 1)),
                  pl.BlockSpec((1, c), const), pl.BlockSpec((1, c), const),
                  pl.BlockSpec((SGU_GROUPS, SGU_CHUNK, SGU_CHUNK), lambda i: (0, 0, 0)),
                  pl.BlockSpec((SGU_CHUNK, SGU_GROUPS), const)],
        out_specs=pl.BlockSpec((tm, c), lambda i: (i, 0)),
        out_shape=jax.ShapeDtypeStruct((t, c), BF16),
        compiler_params=_cparams("parallel"),
        name="sgu",
    )(xn, w_in, w_in, ln_g, ln_b, sgu_w, sgu_bt)


def _gates_kernel(xn_ref, w_ref, o_ref):
    o_ref[...] = _sigmoid(_dot(xn_ref[...], w_ref[...])).astype(BF16)


def _gates(xn, w_in, tm=1024, tn=1024):
    t, d = xn.shape
    n = N_BRANCHES * D_MODEL
    col = GATE_COL // tn
    return pl.pallas_call(
        _gates_kernel,
        grid=(t // tm, n // tn),
        in_specs=[pl.BlockSpec((tm, d), lambda i, j: (i, 0)),
                  pl.BlockSpec((d, tn), lambda i, j: (0, col + j))],
        out_specs=pl.BlockSpec((tm, tn), lambda i, j: (i, j)),
        out_shape=jax.ShapeDtypeStruct((t, n), BF16),
        compiler_params=_cparams("parallel", "arbitrary"),
        name="gates",
    )(xn, w_in)


def _moba_kernel(q_ref, k_ref, vt_ref, km_ref, o_ref, qa_ref, s_ref, p_ref, *, qt):
    seq = q_ref.shape[1]
    n_tiles = seq // qt
    q_blocks = qt // MOBA_BLOCK
    nbp = km_ref.shape[1]
    nb_rows = SUBLANES * pl.cdiv(seq // MOBA_BLOCK, SUBLANES)

    def rows(i):
        return pl.ds(pl.multiple_of(i * qt, qt), qt)

    def gating_stage(i):
        q = q_ref[0, rows(i), :]
        gate = _dot_nt(km_ref[0, 0:nb_rows], q)
        blk_i = lax.broadcasted_iota(jnp.int32, gate.shape, 0)
        qry_i = lax.broadcasted_iota(jnp.int32, gate.shape, 1)
        own = i * q_blocks + sum((qry_i >= r * MOBA_BLOCK).astype(jnp.int32)
                                 for r in range(1, q_blocks))
        blkf = blk_i.astype(F32)
        past = blk_i < own
        g = jnp.where(past, gate, -jnp.inf)
        sel = jnp.zeros(gate.shape, jnp.bool_)
        for _ in range(MOBA_TOPK):
            top = jnp.max(g, axis=0, keepdims=True)
            first = jnp.min(jnp.where(g == top, blkf, F32(nbp)), axis=0, keepdims=True)
            pick = blkf == first
            sel = jnp.logical_or(sel, pick)
            g = jnp.where(pick, -jnp.inf, g)
        allowed = jnp.logical_or(jnp.logical_and(sel, past), blk_i == own)
        bias = jnp.where(allowed, 0.0, MASK_VALUE)
        if nbp > nb_rows:
            bias = jnp.concatenate([bias, jnp.full((nbp - nb_rows, qt), MASK_VALUE, F32)], axis=0)
        bias = bias.T.astype(BF16)
        qa_ref[i % 2, 0] = jnp.concatenate([q, bias], axis=1)
        qa_ref[i % 2, 1] = jnp.concatenate([q, jnp.full(bias.shape, MASK_VALUE, BF16)], axis=1)

    def own_group(i):
        return (i * q_blocks) // GROUP_BLOCKS

    def sweep_len(i):
        return own_group(i) + (1 - own_group(i) % 2) + 1

    def qk_stage(i, t, slot):
        own_g = own_group(i)
        st = pl.multiple_of(jnp.minimum(t, own_g) * KEY_GROUP, KEY_GROUP)
        masked_pass = jnp.logical_and(t == own_g, own_g % 2 == 0)
        s = _dot_nt(k_ref[0, pl.ds(st, KEY_GROUP), :], qa_ref[i % 2, masked_pass.astype(jnp.int32)])
        s_ref[slot] = s
        return jnp.max(s, axis=0, keepdims=True)

    def softmax_stage(slot, m, s_max):
        m_new = jnp.maximum(m, s_max)
        p_ref[slot] = jnp.exp2(s_ref[slot] - m_new).astype(BF16)
        return m_new, jnp.exp2(m - m_new)

    def causal_softmax_stage(slot, m):
        blk = MOBA_BLOCK
        lower = (lax.broadcasted_iota(jnp.int32, (blk, blk), 0)
                 <= lax.broadcasted_iota(jnp.int32, (blk, blk), 1))
        for r in range(q_blocks):
            d = slice(r * blk, (r + 1) * blk)
            s_ref[slot, d, d] = jnp.where(lower, s_ref[slot, d, d], MASK_VALUE)
        s = s_ref[slot]
        m_new = jnp.maximum(m, jnp.max(s, axis=0, keepdims=True))
        p_ref[slot] = jnp.exp2(s - m_new).astype(BF16)
        return jnp.exp2(m - m_new)

    def pv_stage(i, t, slot, acc, alpha):
        return alpha * acc + _dot(vt_ref[0, jnp.minimum(t, own_group(i))], p_ref[slot])

    def fill(i):
        max0 = qk_stage(i, 0, 0)
        max1 = qk_stage(i, 1, 1)
        m, a0 = softmax_stage(0, jnp.full((1, qt), MASK_VALUE, F32), max0)
        return m, a0, max1

    def tile(i, carry):
        m, a0, max1 = carry

        def pair(u, carry):
            m, acc, a0, max1 = carry
            t = 2 + 2 * u
            max0 = qk_stage(i, t, 0)
            m, a1 = softmax_stage(1, m, max1)
            acc = pv_stage(i, t - 2, 0, acc, a0)
            max1 = qk_stage(i, t + 1, 1)
            m, a0 = softmax_stage(0, m, max0)
            acc = pv_stage(i, t - 1, 1, acc, a1)
            return m, acc, a0, max1

        n_all = sweep_len(i)
        acc = jnp.zeros((VT_ROWS, qt), F32)
        m, acc, a0, _ = lax.fori_loop(0, (n_all - 2) // 2, pair, (m, acc, a0, max1))
        nxt = jnp.minimum(i + 1, n_tiles - 1)
        max0 = qk_stage(nxt, 0, 0)
        a1 = causal_softmax_stage(1, m)
        acc = pv_stage(i, n_all - 2, 0, acc, a0)
        max1 = qk_stage(nxt, 1, 1)
        gating_stage(jnp.minimum(i + 2, n_tiles - 1))
        acc = pv_stage(i, n_all - 1, 1, acc, a1)
        m, a0 = softmax_stage(0, jnp.full((1, qt), MASK_VALUE, F32), max0)
        out_t = acc[0:HEAD_DIM] / acc[HEAD_DIM:HEAD_DIM + 1]
        o_ref[0, rows(i), :] = out_t.T.astype(BF16)
        return m, a0, max1

    gating_stage(0)
    if n_tiles > 1:
        gating_stage(1)
    lax.fori_loop(0, n_tiles, tile, fill(0))


def _moba(q, k_aug, vt, kmean, qt=1024):
    b, s, _ = q.shape
    nbp = kmean.shape[1]
    assert qt == KEY_GROUP, "a tile's own blocks are the diagonal of one key group"
    once = pl.Buffered(1)
    return pl.pallas_call(
        functools.partial(_moba_kernel, qt=qt),
        grid=(b, N_HEADS),
        in_specs=[pl.BlockSpec((1, s, HEAD_DIM), lambda bi, h: (bi, 0, h)),
                  pl.BlockSpec((1, s, 2 * HEAD_DIM), lambda bi, h: (bi, 0, h), pipeline_mode=once),
                  pl.BlockSpec((1, s // KEY_GROUP, VT_ROWS, KEY_GROUP), lambda bi, h: (h, bi, 0, 0),
                               pipeline_mode=once),
                  pl.BlockSpec((1, nbp, HEAD_DIM), lambda bi, h: (bi, 0, h))],
        out_specs=pl.BlockSpec((1, s, HEAD_DIM), lambda bi, h: (bi, 0, h)),
        out_shape=jax.ShapeDtypeStruct((b, s, ATTN_WIDTH), BF16),
        scratch_shapes=[pltpu.VMEM((2, 2, qt, HEAD_DIM + nbp), BF16),
                        pltpu.VMEM((2, KEY_GROUP, qt), F32),
                        pltpu.VMEM((2, KEY_GROUP, qt), BF16)],
        compiler_params=_cparams("parallel", "arbitrary"),
        name="moba_attention",
    )(q, k_aug, vt, kmean)


def _merge_kernel(x_ref, a_ref, sb_ref, at_ref, gt_ref, wca_ref, wsg_ref, wat_ref, wo_ref,
                  gpost_ref, o_ref):
    d = D_MODEL
    merged = (gt_ref[:, 0:d].astype(F32) * _dot(a_ref[...], wca_ref[...])
              + gt_ref[:, d:2 * d].astype(F32) * _dot(sb_ref[...], wsg_ref[...])
              + gt_ref[:, 2 * d:3 * d].astype(F32) * _dot(at_ref[...], wat_ref[...]))
    h = _dot(merged.astype(BF16), wo_ref[...])
    o_ref[...] = x_ref[...] + _rms(h, gpost_ref[...])


def _merge(x, a_act, sb, attn, gates, conv_out, sgu_out, attn_out, w_o, g_post, tm=256):
    t, d = x.shape
    const = lambda i: (0, 0)
    wspec = lambda k: pl.BlockSpec((k, d), const, pipeline_mode=pl.Buffered(1))
    return pl.pallas_call(
        _merge_kernel,
        grid=(t // tm,),
        in_specs=[pl.BlockSpec((tm, d), lambda i: (i, 0)),
                  pl.BlockSpec((tm, CONV_CHANNELS), lambda i: (i, 0)),
                  pl.BlockSpec((tm, SGU_WIDTH), lambda i: (i, 0)),
                  pl.BlockSpec((tm, ATTN_WIDTH), lambda i: (i, 0)),
                  pl.BlockSpec((tm, 3 * d), lambda i: (i, 0)),
                  wspec(CONV_CHANNELS), wspec(SGU_WIDTH), wspec(ATTN_WIDTH), wspec(d),
                  pl.BlockSpec((1, d), const)],
        out_specs=pl.BlockSpec((tm, d), lambda i: (i, 0)),
        out_shape=jax.ShapeDtypeStruct((t, d), F32),
        compiler_params=_cparams("parallel"),
        name="merge",
    )(x, a_act, sb, attn, gates, conv_out, sgu_out, attn_out, w_o, g_post)


def _ffn_kernel(x_ref, xh_ref, gpre_ref, wa_ref, wb_ref, dww_ref, dwb_ref, wo_ref, gpost_ref,
                o_ref, hn_ref, acc_ref, *, tiles_per_seq):
    i = pl.program_id(0)
    j = pl.program_id(1)
    tm = x_ref.shape[0]

    @pl.when(j == 0)
    def _():
        halo = _rms(xh_ref[...], gpre_ref[...])
        hn_ref[0:FFN_HALO, :] = jnp.where(i % tiles_per_seq == 0, 0.0, halo).astype(BF16)
        hn_ref[FFN_HALO:, :] = _rms(x_ref[...], gpre_ref[...]).astype(BF16)
        acc_ref[...] = jnp.zeros_like(acc_ref)

    a = _dot(hn_ref[...], wa_ref[...])
    b = _dot(hn_ref[FFN_HALO:, :], wb_ref[...])
    conv = dwb_ref[...] + dww_ref[2:3, :] * a[FFN_HALO:]
    for k in (1, 2):
        conv = conv + dww_ref[2 - k:3 - k, :] * pltpu.roll(a, k, 0)[FFN_HALO:]
    act = (_gelu_tanh(conv) * b).astype(BF16)
    acc_ref[...] += _dot(act, wo_ref[...])

    @pl.when(j == pl.num_programs(1) - 1)
    def _():
        o_ref[...] = x_ref[...] + _rms(acc_ref[...], gpost_ref[...])


def _ffn(x, g_pre, ffn_in, dw_w, dw_b, ffn_out, g_post, seq, tm=512, tf=512):
    t, d = x.shape
    nf = D_FF // tf
    ratio = tm // FFN_HALO
    return pl.pallas_call(
        functools.partial(_ffn_kernel, tiles_per_seq=seq // tm),
        grid=(t // tm, nf),
        in_specs=[pl.BlockSpec((tm, d), lambda i, j: (i, 0)),
                  pl.BlockSpec((FFN_HALO, d), lambda i, j: (jnp.maximum(i * ratio - 1, 0), 0)),
                  pl.BlockSpec((1, d), lambda i, j: (0, 0)),
                  pl.BlockSpec((d, tf), lambda i, j: (0, j)),
                  pl.BlockSpec((d, tf), lambda i, j: (0, j + nf)),
                  pl.BlockSpec((FFN_CONV_KERNEL, tf), lambda i, j: (0, j)),
                  pl.BlockSpec((1, tf), lambda i, j: (0, j)),
                  pl.BlockSpec((tf, d), lambda i, j: (j, 0)),
                  pl.BlockSpec((1, d), lambda i, j: (0, 0))],
        out_specs=pl.BlockSpec((tm, d), lambda i, j: (i, 0)),
        out_shape=jax.ShapeDtypeStruct((t, d), F32),
        scratch_shapes=[pltpu.VMEM((tm + FFN_HALO, d), BF16), pltpu.VMEM((tm, d), F32)],
        compiler_params=_cparams("parallel", "arbitrary"),
        name="conv_ffn",
    )(x, x, g_pre, ffn_in, ffn_in, dw_w, dw_b, ffn_out, g_post)


def _ple_kernel(x_ref, p_ref, g_ref, wg_ref, wp_ref, o_ref):
    x = x_ref[...]
    gate = _sigmoid(_dot(_rms(x, g_ref[...]).astype(BF16), wg_ref[...]))
    o_ref[...] = x + gate * _dot(p_ref[...].astype(BF16), wp_ref[...])


def _ple(x, p, g, w_gate, w_proj, tm=512):
    t, d = x.shape
    const = lambda i: (0, 0)
    return pl.pallas_call(
        _ple_kernel,
        grid=(t // tm,),
        in_specs=[pl.BlockSpec((tm, d), lambda i: (i, 0)),
                  pl.BlockSpec((tm, PLE_DIM), lambda i: (i, 0)),
                  pl.BlockSpec((1, d), const),
                  pl.BlockSpec((d, d), const, pipeline_mode=pl.Buffered(1)),
                  pl.BlockSpec((PLE_DIM, d), const, pipeline_mode=pl.Buffered(1))],
        out_specs=pl.BlockSpec((tm, d), lambda i: (i, 0)),
        out_shape=jax.ShapeDtypeStruct((t, d), F32),
        compiler_params=_cparams("parallel"),
        name="ple",
    )(x, p, g, w_gate, w_proj)


def _rope_tables(seq):
    half = ROPE_DIM // 2
    inv = F32(ROPE_THETA) ** (-jnp.arange(0, ROPE_DIM, 2, dtype=F32) / ROPE_DIM)
    ang = jnp.arange(seq, dtype=jnp.int32).astype(F32)[:, None] * inv[None, :]
    cos, sin = jnp.cos(ang), jnp.sin(ang)
    zeros = jnp.zeros((seq, LANES - ROPE_DIM), F32)
    z16 = jnp.zeros((seq, half), F32)
    cos_t = jnp.concatenate([cos, cos, jnp.ones((seq, LANES - ROPE_DIM), F32)], axis=1)
    sin_lo = jnp.concatenate([-sin, z16, zeros], axis=1)
    sin_hi = jnp.concatenate([z16, sin, zeros], axis=1)
    return cos_t, sin_lo, sin_hi


def kernel(x, p, mix_norm_pre, mix_norm_post, w_in, conv_dw_w, conv_dw_b, conv_norm_g, conv_norm_b, conv_out, sgu_norm_g, sgu_norm_b, sgu_w, sgu_b, sgu_out, attn_out, w_o, ffn_norm_pre, ffn_norm_post, ffn_in, ffn_dw_w, ffn_dw_b, ffn_out, ple_norm, ple_gate, ple_proj):
    bsz, seq, d = x.shape
    depth = w_in.shape[0]
    assert d == D_MODEL and seq % KEY_GROUP == 0
    t = bsz * seq
    nb = seq // MOBA_BLOCK
    nbp = LANES * pl.cdiv(nb, LANES)
    assert nbp == LANES, "block-mask columns share one 256-deep contraction with the head dim"

    cos_t, sin_lo, sin_hi = _rope_tables(seq)
    row = lambda v: v.reshape(1, -1)

    xf = x.reshape(t, d)
    for i in range(depth):
        w = w_in[i].astype(BF16)
        xn = _prenorm(xf, row(mix_norm_pre[i]))
        q, k_aug, vt, kmean = _qkv(xn, w, cos_t, sin_lo, sin_hi, seq)
        a_act = _convmod(xn, w, conv_dw_w[i], row(conv_dw_b[i]),
                         row(conv_norm_g[i]), row(conv_norm_b[i]), seq)
        sb = _sgu(xn, w, row(sgu_norm_g[i]), row(sgu_norm_b[i]), sgu_w[i], sgu_b[i].T)
        gates = _gates(xn, w)
        kmean = kmean.reshape(bsz, nb, ATTN_WIDTH)
        kmean = jnp.pad(kmean, ((0, 0), (0, nbp - nb), (0, 0))).astype(BF16)
        attn = _moba(q.reshape(bsz, seq, ATTN_WIDTH), k_aug.reshape(bsz, seq, 2 * ATTN_WIDTH), vt,
                     kmean).reshape(t, ATTN_WIDTH)
        xf = _merge(xf, a_act, sb, attn, gates, conv_out[i].astype(BF16), sgu_out[i].astype(BF16),
                    attn_out[i].astype(BF16), w_o[i].astype(BF16), row(mix_norm_post[i]))
        xf = _ffn(xf, row(ffn_norm_pre[i]), ffn_in[i].astype(BF16), ffn_dw_w[i], row(ffn_dw_b[i]),
                  ffn_out[i].astype(BF16), row(ffn_norm_post[i]), seq)
        xf = _ple(xf, p[i].reshape(t, PLE_DIM), row(ple_norm[i]), ple_gate[i].astype(BF16),
                  ple_proj[i].astype(BF16))
    return xf.reshape(bsz, seq, d)
```

```python
import functools

import jax
import jax.numpy as jnp
from jax import lax
from jax.experimental import pallas as pl
from jax.experimental.pallas import tpu as pltpu

F32 = jnp.float32
BF16 = jnp.bfloat16

D_MODEL = 2048
PLE_DIM = 256
N_HEADS = 8
HEAD_DIM = 128
ATTN_WIDTH = N_HEADS * HEAD_DIM
ROPE_DIM = HEAD_DIM // 4
ROPE_THETA = 500000.0
MOBA_BLOCK = 256
MOBA_TOPK = 3
CONV_CHANNELS = 512
CONV_KERNEL = 31
SGU_GROUPS = 4
SGU_GROUP_DIM = 128
SGU_WIDTH = SGU_GROUPS * SGU_GROUP_DIM
SGU_CHUNK = 128
D_FF = 5632
FFN_CONV_KERNEL = 3
N_BRANCHES = 3
EPS = 1e-6
CONV_COL = 3 * ATTN_WIDTH
SGU_COL = CONV_COL + 2 * CONV_CHANNELS
GATE_COL = SGU_COL + 2 * SGU_WIDTH

VMEM_LIMIT_BYTES = 56 * 1024 * 1024
LANES = 128
SUBLANES = 8
BF16_SUBLANES = 16
CONV_ROWS = 128
MASK_VALUE = -1e30

CONV_HALO = 32
FFN_HALO = BF16_SUBLANES
VT_ROWS = HEAD_DIM + BF16_SUBLANES
Q_SCALE = HEAD_DIM ** -0.5 * 1.4426950408889634
GROUP_BLOCKS = 4
KEY_GROUP = GROUP_BLOCKS * MOBA_BLOCK


def _cparams(*sem):
    return pltpu.CompilerParams(dimension_semantics=sem, vmem_limit_bytes=VMEM_LIMIT_BYTES)


def _rms(x, g):
    return x * lax.rsqrt(jnp.mean(x * x, axis=-1, keepdims=True) + EPS) * g


def _layer_norm(x, g, b):
    mu = jnp.mean(x, axis=-1, keepdims=True)
    xc = x - mu
    var = jnp.mean(xc * xc, axis=-1, keepdims=True)
    return xc * lax.rsqrt(var + EPS) * g + b


def _gelu_tanh(x):
    c = 0.7978845608028654
    return 0.5 * x * (1.0 + jnp.tanh(c * (x + 0.044715 * (x * x * x))))


def _sigmoid(x):
    return 1.0 / (1.0 + jnp.exp(-x))


def _dot(a, b):
    return jnp.dot(a, b, preferred_element_type=F32)


def _dot_nt(a, b):
    return lax.dot_general(a, b, (((1,), (1,)), ((), ())), preferred_element_type=F32)


def _prenorm_kernel(x_ref, g_ref, o_ref):
    o_ref[...] = _rms(x_ref[...], g_ref[...]).astype(BF16)


def _prenorm(x, g, tm=512):
    t, d = x.shape
    return pl.pallas_call(
        _prenorm_kernel,
        grid=(t // tm,),
        in_specs=[pl.BlockSpec((tm, d), lambda i: (i, 0)),
                  pl.BlockSpec((1, d), lambda i: (0, 0))],
        out_specs=pl.BlockSpec((tm, d), lambda i: (i, 0)),
        out_shape=jax.ShapeDtypeStruct((t, d), BF16),
        compiler_params=_cparams("parallel"),
        name="prenorm",
    )(x, g)


def _rope(xh, cos, sin_lo, sin_hi):
    half = ROPE_DIM // 2
    return (xh * cos + pltpu.roll(xh, LANES - half, 1) * sin_lo
            + pltpu.roll(xh, half, 1) * sin_hi)


def _qkv_kernel(xn_ref, w_ref, cos_ref, slo_ref, shi_ref, q_ref, k_ref, vt_ref, km_ref,
                *, tiles_per_seq):
    i = pl.program_id(0)
    j = pl.program_id(1)
    y = _dot(xn_ref[...], w_ref[...])
    tm = y.shape[0]
    nblk = tm // MOBA_BLOCK
    heads = [slice(h * HEAD_DIM, (h + 1) * HEAD_DIM) for h in range(N_HEADS)]

    @pl.when(j == 0)
    def _():
        cos, slo, shi = cos_ref[...], slo_ref[...], shi_ref[...]
        for sl in heads:
            q_ref[:, sl] = (_rope(y[:, sl], cos, slo, shi) * F32(Q_SCALE)).astype(BF16)

    @pl.when(j == 1)
    def _():
        cos, slo, shi = cos_ref[...], slo_ref[...], shi_ref[...]
        row = lax.broadcasted_iota(jnp.int32, (tm, LANES), 0)
        lane = lax.broadcasted_iota(jnp.int32, (tm, LANES), 1)
        first_blk = (i % tiles_per_seq) * nblk
        onehot = jnp.zeros((tm, LANES), F32)
        for c in range(nblk):
            in_c = jnp.logical_and(row >= c * MOBA_BLOCK, row < (c + 1) * MOBA_BLOCK)
            onehot = jnp.where(jnp.logical_and(in_c, lane == first_blk + c), 1.0, onehot)
        onehot = onehot.astype(BF16)
        for h, sl in enumerate(heads):
            r = _rope(y[:, sl], cos, slo, shi)
            k_ref[:, 2 * h * HEAD_DIM:(2 * h + 1) * HEAD_DIM] = r.astype(BF16)
            k_ref[:, (2 * h + 1) * HEAD_DIM:(2 * h + 2) * HEAD_DIM] = onehot
            for c in range(nblk):
                blk = r[c * MOBA_BLOCK:(c + 1) * MOBA_BLOCK]
                km_ref[0, c:c + 1, sl] = jnp.mean(blk, axis=0, keepdims=True)

    @pl.when(j == 2)
    def _():
        for h, sl in enumerate(heads):
            vt_ref[h, 0, 0:HEAD_DIM, :] = y[:, sl].T.astype(BF16)
            vt_ref[h, 0, HEAD_DIM:, :] = jnp.ones((VT_ROWS - HEAD_DIM, tm), BF16)


def _qkv(xn, w_in, layer, cos, slo, shi, seq, tm=1024):
    t, d = xn.shape
    tiles_per_seq = seq // tm
    nblk = tm // MOBA_BLOCK
    per_group = KEY_GROUP // tm
    assert per_group * tm == KEY_GROUP
    rope_spec =pl.BlockSpec((tm, LANES), lambda i, j: (i % tiles_per_seq, 0))
    return pl.pallas_call(
        functools.partial(_qkv_kernel, tiles_per_seq=tiles_per_seq),
        grid=(t // tm, 3),
        in_specs=[pl.BlockSpec((tm, d), lambda i, j: (i, 0)),
                  pl.BlockSpec((None, d, ATTN_WIDTH), lambda i, j: (layer, 0, j)),
                  rope_spec, rope_spec, rope_spec],
        out_specs=[pl.BlockSpec((tm, ATTN_WIDTH), lambda i, j: (i, 0)),
                   pl.BlockSpec((tm, 2 * ATTN_WIDTH), lambda i, j: (i, 0)),
                   pl.BlockSpec((N_HEADS, 1, VT_ROWS, tm), lambda i, j: (0, i // per_group, 0, i % per_group)),
                   pl.BlockSpec((1, nblk, ATTN_WIDTH), lambda i, j: (i, 0, 0))],
        out_shape=[jax.ShapeDtypeStruct((t, ATTN_WIDTH), BF16),
                   jax.ShapeDtypeStruct((t, 2 * ATTN_WIDTH), BF16),
                   jax.ShapeDtypeStruct((N_HEADS, t // KEY_GROUP, VT_ROWS, KEY_GROUP), BF16),
                   jax.ShapeDtypeStruct((t // tm, nblk, ATTN_WIDTH), F32)],
        compiler_params=_cparams("parallel", "arbitrary"),
        name="qkv_proj",
    )(xn, w_in, cos, slo, shi)


def _convmod_kernel(xn_ref, xh_ref, wa_ref, wg_ref, dww_ref, dwb_ref, lng_ref, lnb_ref,
                    o_ref, a_ref, sh_ref, *, tiles_per_seq):
    i = pl.program_id(0)
    tm = xn_ref.shape[0]

    def glu(xn):
        return _dot(xn, wa_ref[...]) * _sigmoid(_dot(xn, wg_ref[...]))

    halo = glu(xh_ref[...])
    a_ref[0:CONV_HALO, :] = jnp.where(i % tiles_per_seq == 0, 0.0, halo)
    a_ref[CONV_HALO:, :] = glu(xn_ref[...])

    rb = CONV_ROWS
    first = CONV_HALO - (CONV_KERNEL - 1)
    for r in range(tm // rb):
        cols = []
        for c in range(CONV_CHANNELS // LANES):
            cs = slice(c * LANES, (c + 1) * LANES)
            acc = dwb_ref[:, cs]
            for j in range(SUBLANES):
                rows = rb if j == 0 else rb + SUBLANES
                part = None
                for k in range(CONV_KERNEL):
                    if (first + k) % SUBLANES == j:
                        term = dww_ref[k:k + 1, cs] * a_ref[pl.ds(r * rb + first + k - j, rows), cs]
                        part = term if part is None else part + term
                if j > 0:
                    sh_ref[j] = part
                    part = sh_ref[j, pl.ds(j, rb), :]
                acc = acc + part
            cols.append(acc)
        y = jnp.concatenate(cols, axis=1)
        y = _layer_norm(y, lng_ref[...], lnb_ref[...])
        o_ref[r * rb:(r + 1) * rb, :] = (y * _sigmoid(y)).astype(BF16)


def _convmod(xn, w_in, layer, dw_w, dw_b, ln_g, ln_b, seq, tm=512):
    t, d = xn.shape
    c = CONV_CHANNELS
    col = CONV_COL // c
    ratio = tm // CONV_HALO
    const = lambda i: (0, 0)
    return pl.pallas_call(
        functools.partial(_convmod_kernel, tiles_per_seq=seq // tm),
        grid=(t // tm,),
        in_specs=[pl.BlockSpec((tm, d), lambda i: (i, 0)),
                  pl.BlockSpec((CONV_HALO, d), lambda i: (jnp.maximum(i * ratio - 1, 0), 0)),
                  pl.BlockSpec((None, d, c), lambda i: (layer, 0, col)),
                  pl.BlockSpec((None, d, c), lambda i: (layer, 0, col + 1)),
                  pl.BlockSpec((CONV_KERNEL, c), const), pl.BlockSpec((1, c), const),
                  pl.BlockSpec((1, c), const), pl.BlockSpec((1, c), const)],
        out_specs=pl.BlockSpec((tm, c), lambda i: (i, 0)),
        out_shape=jax.ShapeDtypeStruct((t, c), BF16),
        scratch_shapes=[pltpu.VMEM((tm + CONV_HALO, c), F32),
                        pltpu.VMEM((SUBLANES, CONV_ROWS + SUBLANES, LANES), F32)],
        compiler_params=_cparams("parallel"),
        name="conv_module",
    )(xn, xn, w_in, w_in, dw_w, dw_b, ln_g, ln_b)


def _sgu_kernel(xn_ref, wu_ref, wv_ref, lng_ref, lnb_ref, sw_ref, sbt_ref, o_ref):
    tm = xn_ref.shape[0]
    xn = xn_ref[...]
    u = _gelu_tanh(_dot(xn, wu_ref[...]))
    v = _layer_norm(_gelu_tanh(_dot(xn, wv_ref[...])), lng_ref[...], lnb_ref[...]).astype(BF16)
    row = lax.broadcasted_iota(jnp.int32, (SGU_CHUNK, SGU_CHUNK), 0)
    col = lax.broadcasted_iota(jnp.int32, (SGU_CHUNK, SGU_CHUNK), 1)
    for g in range(SGU_GROUPS):
        gs = slice(g * SGU_GROUP_DIM, (g + 1) * SGU_GROUP_DIM)
        w = jnp.where(col <= row, sw_ref[g], 0.0).astype(BF16)
        bias = sbt_ref[:, g:g + 1]
        for n in range(tm // SGU_CHUNK):
            rs = slice(n * SGU_CHUNK, (n + 1) * SGU_CHUNK)
            mixed = _dot(w, v[rs, gs]) + bias
            o_ref[rs, gs] = (u[rs, gs] * mixed).astype(BF16)


def _sgu(xn, w_in, layer, ln_g, ln_b, sgu_w, sgu_bt, tm=512):
    t, d = xn.shape
    c = SGU_WIDTH
    col = SGU_COL // c
    const = lambda i: (0, 0)
    return pl.pallas_call(
        _sgu_kernel,
        grid=(t // tm,),
        in_specs=[pl.BlockSpec((tm, d), lambda i: (i, 0)),
                  pl.BlockSpec((None, d, c), lambda i: (layer, 0, col)),
                  pl.BlockSpec((None, d, c), lambda i: (layer, 0, col + 1)),
                  pl.BlockSpec((1, c), const), pl.BlockSpec((1, c), const),
                  pl.BlockSpec((SGU_GROUPS, SGU_CHUNK, SGU_CHUNK), lambda i: (0, 0, 0)),
                  pl.BlockSpec((SGU_CHUNK, SGU_GROUPS), const)],
        out_specs=pl.BlockSpec((tm, c), lambda i: (i, 0)),
        out_shape=jax.ShapeDtypeStruct((t, c), BF16),
        compiler_params=_cparams("parallel"),
        name="sgu",
    )(xn, w_in, w_in, ln_g, ln_b, sgu_w, sgu_bt)


def _gates_kernel(xn_ref, w_ref, o_ref):
    o_ref[...] = _sigmoid(_dot(xn_ref[...], w_ref[...])).astype(BF16)


def _gates(xn, w_in, layer, tm=1024, tn=1024):
    t, d = xn.shape
    n = N_BRANCHES * D_MODEL
    col = GATE_COL // tn
    return pl.pallas_call(
        _gates_kernel,
        grid=(t // tm, n // tn),
        in_specs=[pl.BlockSpec((tm, d), lambda i, j: (i, 0)),
                  pl.BlockSpec((None, d, tn), lambda i, j: (layer, 0, col + j))],
        out_specs=pl.BlockSpec((tm, tn), lambda i, j: (i, j)),
        out_shape=jax.ShapeDtypeStruct((t, n), BF16),
        compiler_params=_cparams("parallel", "arbitrary"),
        name="gates",
    )(xn, w_in)


def _moba_kernel(q_ref, k_ref, vt_ref, km_ref, o_ref, qa_ref, s_ref, p_ref, *, qt):
    seq = q_ref.shape[1]
    n_tiles = seq // qt
    q_blocks = qt // MOBA_BLOCK
    nbp = km_ref.shape[1]
    nb_rows = SUBLANES * pl.cdiv(seq // MOBA_BLOCK, SUBLANES)

    def rows(i):
        return pl.ds(pl.multiple_of(i * qt, qt), qt)

    def gating_stage(i):
        q = q_ref[0, rows(i), :]
        gate = _dot_nt(km_ref[0, 0:nb_rows], q)
        blk_i = lax.broadcasted_iota(jnp.int32, gate.shape, 0)
        qry_i = lax.broadcasted_iota(jnp.int32, gate.shape, 1)
        own = i * q_blocks + sum((qry_i >= r * MOBA_BLOCK).astype(jnp.int32)
                                 for r in range(1, q_blocks))
        blkf = blk_i.astype(F32)
        past = blk_i < own
        g = jnp.where(past, gate, -jnp.inf)
        sel = jnp.zeros(gate.shape, jnp.bool_)
        for _ in range(MOBA_TOPK):
            top = jnp.max(g, axis=0, keepdims=True)
            first = jnp.min(jnp.where(g == top, blkf, F32(nbp)), axis=0, keepdims=True)
            pick = blkf == first
            sel = jnp.logical_or(sel, pick)
            g = jnp.where(pick, -jnp.inf, g)
        allowed = jnp.logical_or(jnp.logical_and(sel, past), blk_i == own)
        bias = jnp.where(allowed, 0.0, MASK_VALUE)
        if nbp > nb_rows:
            bias = jnp.concatenate([bias, jnp.full((nbp - nb_rows, qt), MASK_VALUE, F32)], axis=0)
        bias = bias.T.astype(BF16)
        qa_ref[i % 2, 0] = jnp.concatenate([q, bias], axis=1)
        qa_ref[i % 2, 1] = jnp.concatenate([q, jnp.full(bias.shape, MASK_VALUE, BF16)], axis=1)

    def own_group(i):
        return (i * q_blocks) // GROUP_BLOCKS

    def sweep_len(i):
        return own_group(i) + (1 - own_group(i) % 2) + 1

    def qk_stage(i, t, slot):
        own_g = own_group(i)
        st = pl.multiple_of(jnp.minimum(t, own_g) * KEY_GROUP, KEY_GROUP)
        masked_pass = jnp.logical_and(t == own_g, own_g % 2 == 0)
        s = _dot_nt(k_ref[0, pl.ds(st, KEY_GROUP), :], qa_ref[i % 2, masked_pass.astype(jnp.int32)])
        s_ref[slot] = s
        return jnp.max(s, axis=0, keepdims=True)

    def softmax_stage(slot, m, s_max):
        m_new = jnp.maximum(m, s_max)
        p_ref[slot] = jnp.exp2(s_ref[slot] - m_new).astype(BF16)
        return m_new, jnp.exp2(m - m_new)

    def causal_softmax_stage(slot, m):
        blk = MOBA_BLOCK
        lower = (lax.broadcasted_iota(jnp.int32, (blk, blk), 0)
                 <= lax.broadcasted_iota(jnp.int32, (blk, blk), 1))
        for r in range(q_blocks):
            d = slice(r * blk, (r + 1) * blk)
            s_ref[slot, d, d] = jnp.where(lower, s_ref[slot, d, d], MASK_VALUE)
        s = s_ref[slot]
        m_new = jnp.maximum(m, jnp.max(s, axis=0, keepdims=True))
        p_ref[slot] = jnp.exp2(s - m_new).astype(BF16)
        return jnp.exp2(m - m_new)

    def pv_stage(i, t, slot, acc, alpha):
        return alpha * acc + _dot(vt_ref[0, jnp.minimum(t, own_group(i))], p_ref[slot])

    def fill(i):
        max0 = qk_stage(i, 0, 0)
        max1 = qk_stage(i, 1, 1)
        m, a0 = softmax_stage(0, jnp.full((1, qt), MASK_VALUE, F32), max0)
        return m, a0, max1

    def tile(i, carry):
        m, a0, max1 = carry

        def pair(u, carry):
            m, acc, a0, max1 = carry
            t = 2 + 2 * u
            max0 = qk_stage(i, t, 0)
            m, a1 = softmax_stage(1, m, max1)
            acc = pv_stage(i, t - 2, 0, acc, a0)
            max1 = qk_stage(i, t + 1, 1)
            m, a0 = softmax_stage(0, m, max0)
            acc = pv_stage(i, t - 1, 1, acc, a1)
            return m, acc, a0, max1

        n_all = sweep_len(i)
        acc = jnp.zeros((VT_ROWS, qt), F32)
        m, acc, a0, _ = lax.fori_loop(0, (n_all - 2) // 2, pair, (m, acc, a0, max1))
        nxt = jnp.minimum(i + 1, n_tiles - 1)
        max0 = qk_stage(nxt, 0, 0)
        a1 = causal_softmax_stage(1, m)
        acc = pv_stage(i, n_all - 2, 0, acc, a0)
        max1 = qk_stage(nxt, 1, 1)
        gating_stage(jnp.minimum(i + 2, n_tiles - 1))
        acc = pv_stage(i, n_all - 1, 1, acc, a1)
        m, a0 = softmax_stage(0, jnp.full((1, qt), MASK_VALUE, F32), max0)
        out_t = acc[0:HEAD_DIM] / acc[HEAD_DIM:HEAD_DIM + 1]
        o_ref[0, rows(i), :] = out_t.T.astype(BF16)
        return m, a0, max1

    gating_stage(0)
    if n_tiles > 1:
        gating_stage(1)
    lax.fori_loop(0, n_tiles, tile, fill(0))


def _moba(q, k_aug, vt, kmean, qt=1024):
    b, s, _ = q.shape
    nbp = kmean.shape[1]
    assert qt == KEY_GROUP, "a tile's own blocks are the diagonal of one key group"
    once = pl.Buffered(1)
    return pl.pallas_call(
        functools.partial(_moba_kernel, qt=qt),
        grid=(b, N_HEADS),
        in_specs=[pl.BlockSpec((1, s, HEAD_DIM), lambda bi, h: (bi, 0, h)),
                  pl.BlockSpec((1, s, 2 * HEAD_DIM), lambda bi, h: (bi, 0, h), pipeline_mode=once),
                  pl.BlockSpec((1, s // KEY_GROUP, VT_ROWS, KEY_GROUP), lambda bi, h: (h, bi, 0, 0),
                               pipeline_mode=once),
                  pl.BlockSpec((1, nbp, HEAD_DIM), lambda bi, h: (bi, 0, h))],
        out_specs=pl.BlockSpec((1, s, HEAD_DIM), lambda bi, h: (bi, 0, h)),
        out_shape=jax.ShapeDtypeStruct((b, s, ATTN_WIDTH), BF16),
        scratch_shapes=[pltpu.VMEM((2, 2, qt, HEAD_DIM + nbp), BF16),
                        pltpu.VMEM((2, KEY_GROUP, qt), F32),
                        pltpu.VMEM((2, KEY_GROUP, qt), BF16)],
        compiler_params=_cparams("parallel", "arbitrary"),
        name="moba_attention",
    )(q, k_aug, vt, kmean)


def _merge_kernel(x_ref, a_ref, sb_ref, at_ref, gt_ref, wca_ref, wsg_ref, wat_ref, wo_ref,
                  gpost_ref, o_ref):
    d = D_MODEL
    merged = (gt_ref[:, 0:d].astype(F32) * _dot(a_ref[...], wca_ref[...])
              + gt_ref[:, d:2 * d].astype(F32) * _dot(sb_ref[...], wsg_ref[...])
              + gt_ref[:, 2 * d:3 * d].astype(F32) * _dot(at_ref[...], wat_ref[...]))
    h = _dot(merged.astype(BF16), wo_ref[...])
    o_ref[...] = x_ref[...] + _rms(h, gpost_ref[...])


def _merge(x, a_act, sb, attn, gates, conv_out, sgu_out, attn_out, w_o, layer, g_post, tm=256):
    t, d = x.shape
    const = lambda i: (0, 0)
    wspec = lambda k: pl.BlockSpec((None, k, d), lambda i: (layer, 0, 0), pipeline_mode=pl.Buffered(1))
    return pl.pallas_call(
        _merge_kernel,
        grid=(t // tm,),
        in_specs=[pl.BlockSpec((tm, d), lambda i: (i, 0)),
                  pl.BlockSpec((tm, CONV_CHANNELS), lambda i: (i, 0)),
                  pl.BlockSpec((tm, SGU_WIDTH), lambda i: (i, 0)),
                  pl.BlockSpec((tm, ATTN_WIDTH), lambda i: (i, 0)),
                  pl.BlockSpec((tm, 3 * d), lambda i: (i, 0)),
                  wspec(CONV_CHANNELS), wspec(SGU_WIDTH), wspec(ATTN_WIDTH), wspec(d),
                  pl.BlockSpec((1, d), const)],
        out_specs=pl.BlockSpec((tm, d), lambda i: (i, 0)),
        out_shape=jax.ShapeDtypeStruct((t, d), F32),
        compiler_params=_cparams("parallel"),
        name="merge",
    )(x, a_act, sb, attn, gates, conv_out, sgu_out, attn_out, w_o, g_post)


def _ffn_kernel(x_ref, xh_ref, gpre_ref, wa_ref, wb_ref, dww_ref, dwb_ref, wo_ref, gpost_ref,
                o_ref, hn_ref, acc_ref, *, tiles_per_seq):
    i = pl.program_id(0)
    j = pl.program_id(1)
    tm = x_ref.shape[0]

    @pl.when(j == 0)
    def _():
        halo = _rms(xh_ref[...], gpre_ref[...])
        hn_ref[0:FFN_HALO, :] = jnp.where(i % tiles_per_seq == 0, 0.0, halo).astype(BF16)
        hn_ref[FFN_HALO:, :] = _rms(x_ref[...], gpre_ref[...]).astype(BF16)
        acc_ref[...] = jnp.zeros_like(acc_ref)

    a = _dot(hn_ref[...], wa_ref[...])
    b = _dot(hn_ref[FFN_HALO:, :], wb_ref[...])
    conv = dwb_ref[...] + dww_ref[2:3, :] * a[FFN_HALO:]
    for k in (1, 2):
        conv = conv + dww_ref[2 - k:3 - k, :] * pltpu.roll(a, k, 0)[FFN_HALO:]
    act = (_gelu_tanh(conv) * b).astype(BF16)
    acc_ref[...] += _dot(act, wo_ref[...])

    @pl.when(j == pl.num_programs(1) - 1)
    def _():
        o_ref[...] = x_ref[...] + _rms(acc_ref[...], gpost_ref[...])


def _ffn(x, g_pre, ffn_in, dw_w, dw_b, ffn_out, layer, g_post, seq, tm=512, tf=512):
    t, d = x.shape
    nf = D_FF // tf
    ratio = tm // FFN_HALO
    return pl.pallas_call(
        functools.partial(_ffn_kernel, tiles_per_seq=seq // tm),
        grid=(t // tm, nf),
        in_specs=[pl.BlockSpec((tm, d), lambda i, j: (i, 0)),
                  pl.BlockSpec((FFN_HALO, d), lambda i, j: (jnp.maximum(i * ratio - 1, 0), 0)),
                  pl.BlockSpec((1, d), lambda i, j: (0, 0)),
                  pl.BlockSpec((None, d, tf), lambda i, j: (layer, 0, j)),
                  pl.BlockSpec((None, d, tf), lambda i, j: (layer, 0, j + nf)),
                  pl.BlockSpec((FFN_CONV_KERNEL, tf), lambda i, j: (0, j)),
                  pl.BlockSpec((1, tf), lambda i, j: (0, j)),
                  pl.BlockSpec((None, tf, d), lambda i, j: (layer, j, 0)),
                  pl.BlockSpec((1, d), lambda i, j: (0, 0))],
        out_specs=pl.BlockSpec((tm, d), lambda i, j: (i, 0)),
        out_shape=jax.ShapeDtypeStruct((t, d), F32),
        scratch_shapes=[pltpu.VMEM((tm + FFN_HALO, d), BF16), pltpu.VMEM((tm, d), F32)],
        compiler_params=_cparams("parallel", "arbitrary"),
        name="conv_ffn",
    )(x, x, g_pre, ffn_in, ffn_in, dw_w, dw_b, ffn_out, g_post)


def _ple_kernel(x_ref, p_ref, g_ref, wg_ref, wp_ref, o_ref):
    x = x_ref[...]
    gate = _sigmoid(_dot(_rms(x, g_ref[...]).astype(BF16), wg_ref[...]))
    o_ref[...] = x + gate * _dot(p_ref[...].astype(BF16), wp_ref[...])


def _ple(x, p, g, w_gate, w_proj, layer, tm=512):
    t, d = x.shape
    const = lambda i: (0, 0)
    return pl.pallas_call(
        _ple_kernel,
        grid=(t // tm,),
        in_specs=[pl.BlockSpec((tm, d), lambda i: (i, 0)),
                  pl.BlockSpec((None, tm, PLE_DIM), lambda i: (layer, i, 0)),
                  pl.BlockSpec((1, d), const),
                  pl.BlockSpec((None, d, d), lambda i: (layer, 0, 0), pipeline_mode=pl.Buffered(1)),
                  pl.BlockSpec((None, PLE_DIM, d), lambda i: (layer, 0, 0), pipeline_mode=pl.Buffered(1))],
        out_specs=pl.BlockSpec((tm, d), lambda i: (i, 0)),
        out_shape=jax.ShapeDtypeStruct((t, d), F32),
        compiler_params=_cparams("parallel"),
        name="ple",
    )(x, p, g, w_gate, w_proj)


def _rope_tables(seq):
    half = ROPE_DIM // 2
    inv = F32(ROPE_THETA) ** (-jnp.arange(0, ROPE_DIM, 2, dtype=F32) / ROPE_DIM)
    ang = jnp.arange(seq, dtype=jnp.int32).astype(F32)[:, None] * inv[None, :]
    cos, sin = jnp.cos(ang), jnp.sin(ang)
    zeros = jnp.zeros((seq, LANES - ROPE_DIM), F32)
    z16 = jnp.zeros((seq, half), F32)
    cos_t = jnp.concatenate([cos, cos, jnp.ones((seq, LANES - ROPE_DIM), F32)], axis=1)
    sin_lo = jnp.concatenate([-sin, z16, zeros], axis=1)
    sin_hi = jnp.concatenate([z16, sin, zeros], axis=1)
    return cos_t, sin_lo, sin_hi


def kernel(x, p, mix_norm_pre, mix_norm_post, w_in, conv_dw_w, conv_dw_b, conv_norm_g, conv_norm_b, conv_out, sgu_norm_g, sgu_norm_b, sgu_w, sgu_b, sgu_out, attn_out, w_o, ffn_norm_pre, ffn_norm_post, ffn_in, ffn_dw_w, ffn_dw_b, ffn_out, ple_norm, ple_gate, ple_proj):
    bsz, seq, d = x.shape
    depth = w_in.shape[0]
    assert d == D_MODEL and seq % KEY_GROUP == 0
    t = bsz * seq
    nb = seq // MOBA_BLOCK
    nbp = LANES * pl.cdiv(nb, LANES)
    assert nbp == LANES, "block-mask columns share one 256-deep contraction with the head dim"

    cos_t, sin_lo, sin_hi = _rope_tables(seq)
    row = lambda v: v.reshape(1, -1)

    w_in, conv_out, sgu_out, attn_out, w_o, ffn_in, ffn_out, ple_gate, ple_proj = (
        a.astype(BF16) for a in (w_in, conv_out, sgu_out, attn_out, w_o, ffn_in, ffn_out,
                                 ple_gate, ple_proj))
    p = p.reshape(depth, t, PLE_DIM)
    xf = x.reshape(t, d)
    for i in range(depth):
        xn = _prenorm(xf, row(mix_norm_pre[i]))
        q, k_aug, vt, kmean = _qkv(xn, w_in, i, cos_t, sin_lo, sin_hi, seq)
        a_act = _convmod(xn, w_in, i, conv_dw_w[i], row(conv_dw_b[i]),
                         row(conv_norm_g[i]), row(conv_norm_b[i]), seq)
        sb = _sgu(xn, w_in, i, row(sgu_norm_g[i]), row(sgu_norm_b[i]), sgu_w[i], sgu_b[i].T)
        gates = _gates(xn, w_in, i)
        kmean = kmean.reshape(bsz, nb, ATTN_WIDTH)
        kmean = jnp.pad(kmean, ((0, 0), (0, nbp - nb), (0, 0))).astype(BF16)
        attn = _moba(q.reshape(bsz, seq, ATTN_WIDTH), k_aug.reshape(bsz, seq, 2 * ATTN_WIDTH), vt,
                     kmean).reshape(t, ATTN_WIDTH)
        xf = _merge(xf, a_act, sb, attn, gates, conv_out, sgu_out, attn_out, w_o, i,
                    row(mix_norm_post[i]))
        xf = _ffn(xf, row(ffn_norm_pre[i]), ffn_in, ffn_dw_w[i], row(ffn_dw_b[i]), ffn_out, i,
                  row(ffn_norm_post[i]), seq)
        xf = _ple(xf, p, row(ple_norm[i]), ple_gate, ple_proj, i)
    return xf.reshape(bsz, seq, d)
```

```python
import functools

import jax
import jax.numpy as jnp
from jax import lax
from jax.experimental import pallas as pl
from jax.experimental.pallas import tpu as pltpu

F32 = jnp.float32
BF16 = jnp.bfloat16

D_MODEL = 2048
PLE_DIM = 256
N_HEADS = 8
HEAD_DIM = 128
ATTN_WIDTH = N_HEADS * HEAD_DIM
ROPE_DIM = HEAD_DIM // 4
ROPE_THETA = 500000.0
MOBA_BLOCK = 256
MOBA_TOPK = 3
CONV_CHANNELS = 512
CONV_KERNEL = 31
SGU_GROUPS = 4
SGU_GROUP_DIM = 128
SGU_WIDTH = SGU_GROUPS * SGU_GROUP_DIM
SGU_CHUNK = 128
D_FF = 5632
FFN_CONV_KERNEL = 3
N_BRANCHES = 3
EPS = 1e-6
CONV_COL = 3 * ATTN_WIDTH
SGU_COL = CONV_COL + 2 * CONV_CHANNELS
GATE_COL = SGU_COL + 2 * SGU_WIDTH

VMEM_LIMIT_BYTES = 56 * 1024 * 1024
LANES = 128
SUBLANES = 8
BF16_SUBLANES = 16
CONV_ROWS = 128
MASK_VALUE = -1e30

CONV_HALO = 32
FFN_HALO = BF16_SUBLANES
VT_ROWS = HEAD_DIM + BF16_SUBLANES
Q_SCALE = HEAD_DIM ** -0.5 * 1.4426950408889634
GROUP_BLOCKS = 4
KEY_GROUP = GROUP_BLOCKS * MOBA_BLOCK


def _cparams(*sem):
    return pltpu.CompilerParams(dimension_semantics=sem, vmem_limit_bytes=VMEM_LIMIT_BYTES)


def _rms(x, g):
    return x * lax.rsqrt(jnp.mean(x * x, axis=-1, keepdims=True) + EPS) * g


def _layer_norm(x, g, b):
    mu = jnp.mean(x, axis=-1, keepdims=True)
    xc = x - mu
    var = jnp.mean(xc * xc, axis=-1, keepdims=True)
    return xc * lax.rsqrt(var + EPS) * g + b


def _gelu_tanh(x):
    c = 0.7978845608028654
    return 0.5 * x * (1.0 + jnp.tanh(c * (x + 0.044715 * (x * x * x))))


def _sigmoid(x):
    return 1.0 / (1.0 + jnp.exp(-x))


def _dot(a, b):
    return jnp.dot(a, b, preferred_element_type=F32)


def _dot_nt(a, b):
    return lax.dot_general(a, b, (((1,), (1,)), ((), ())), preferred_element_type=F32)


def _prenorm_kernel(x_ref, g_ref, o_ref):
    o_ref[...] = _rms(x_ref[...], g_ref[...]).astype(BF16)


def _prenorm(x, g, tm=512):
    t, d = x.shape
    return pl.pallas_call(
        _prenorm_kernel,
        grid=(t // tm,),
        in_specs=[pl.BlockSpec((tm, d), lambda i: (i, 0)),
                  pl.BlockSpec((1, d), lambda i: (0, 0))],
        out_specs=pl.BlockSpec((tm, d), lambda i: (i, 0)),
        out_shape=jax.ShapeDtypeStruct((t, d), BF16),
        compiler_params=_cparams("parallel"),
        name="prenorm",
    )(x, g)


def _rope(xh, cos, sin_lo, sin_hi):
    half = ROPE_DIM // 2
    return (xh * cos + pltpu.roll(xh, LANES - half, 1) * sin_lo
            + pltpu.roll(xh, half, 1) * sin_hi)


def _qkv_kernel(xn_ref, w_ref, cos_ref, slo_ref, shi_ref, q_ref, k_ref, vt_ref, km_ref,
                *, tiles_per_seq):
    i = pl.program_id(0)
    j = pl.program_id(1)
    y = _dot(xn_ref[...], w_ref[...])
    tm = y.shape[0]
    nblk = tm // MOBA_BLOCK
    heads = [slice(h * HEAD_DIM, (h + 1) * HEAD_DIM) for h in range(N_HEADS)]

    @pl.when(j == 0)
    def _():
        cos, slo, shi = cos_ref[...], slo_ref[...], shi_ref[...]
        for sl in heads:
            q_ref[:, sl] = (_rope(y[:, sl], cos, slo, shi) * F32(Q_SCALE)).astype(BF16)

    @pl.when(j == 1)
    def _():
        cos, slo, shi = cos_ref[...], slo_ref[...], shi_ref[...]
        row = lax.broadcasted_iota(jnp.int32, (tm, LANES), 0)
        lane = lax.broadcasted_iota(jnp.int32, (tm, LANES), 1)
        first_blk = (i % tiles_per_seq) * nblk
        onehot = jnp.zeros((tm, LANES), F32)
        for c in range(nblk):
            in_c = jnp.logical_and(row >= c * MOBA_BLOCK, row < (c + 1) * MOBA_BLOCK)
            onehot = jnp.where(jnp.logical_and(in_c, lane == first_blk + c), 1.0, onehot)
        onehot = onehot.astype(BF16)
        for h, sl in enumerate(heads):
            r = _rope(y[:, sl], cos, slo, shi)
            k_ref[:, 2 * h * HEAD_DIM:(2 * h + 1) * HEAD_DIM] = r.astype(BF16)
            k_ref[:, (2 * h + 1) * HEAD_DIM:(2 * h + 2) * HEAD_DIM] = onehot
            for c in range(nblk):
                blk = r[c * MOBA_BLOCK:(c + 1) * MOBA_BLOCK]
                km_ref[0, c:c + 1, sl] = jnp.mean(blk, axis=0, keepdims=True)

    @pl.when(j == 2)
    def _():
        for h, sl in enumerate(heads):
            vt_ref[h, 0, 0:HEAD_DIM, :] = y[:, sl].T.astype(BF16)
            vt_ref[h, 0, HEAD_DIM:, :] = jnp.ones((VT_ROWS - HEAD_DIM, tm), BF16)


def _qkv(xn, w_in, layer, cos, slo, shi, seq, tm=1024):
    t, d = xn.shape
    tiles_per_seq = seq // tm
    nblk = tm // MOBA_BLOCK
    per_group = KEY_GROUP // tm
    assert per_group * tm == KEY_GROUP
    rope_spec =pl.BlockSpec((tm, LANES), lambda i, j: (i % tiles_per_seq, 0))
    return pl.pallas_call(
        functools.partial(_qkv_kernel, tiles_per_seq=tiles_per_seq),
        grid=(t // tm, 3),
        in_specs=[pl.BlockSpec((tm, d), lambda i, j: (i, 0)),
                  pl.BlockSpec((None, d, ATTN_WIDTH), lambda i, j: (layer, 0, j)),
                  rope_spec, rope_spec, rope_spec],
        out_specs=[pl.BlockSpec((tm, ATTN_WIDTH), lambda i, j: (i, 0)),
                   pl.BlockSpec((tm, 2 * ATTN_WIDTH), lambda i, j: (i, 0)),
                   pl.BlockSpec((N_HEADS, 1, VT_ROWS, tm), lambda i, j: (0, i // per_group, 0, i % per_group)),
                   pl.BlockSpec((1, nblk, ATTN_WIDTH), lambda i, j: (i, 0, 0))],
        out_shape=[jax.ShapeDtypeStruct((t, ATTN_WIDTH), BF16),
                   jax.ShapeDtypeStruct((t, 2 * ATTN_WIDTH), BF16),
                   jax.ShapeDtypeStruct((N_HEADS, t // KEY_GROUP, VT_ROWS, KEY_GROUP), BF16),
                   jax.ShapeDtypeStruct((t // tm, nblk, ATTN_WIDTH), F32)],
        compiler_params=_cparams("parallel", "arbitrary"),
        name="qkv_proj",
    )(xn, w_in, cos, slo, shi)


def _convmod_kernel(xn_ref, xh_ref, wa_ref, wg_ref, dww_ref, dwb_ref, lng_ref, lnb_ref,
                    o_ref, a_ref, sh_ref, *, tiles_per_seq):
    i = pl.program_id(0)
    tm = xn_ref.shape[0]

    def glu(xn):
        return _dot(xn, wa_ref[...]) * _sigmoid(_dot(xn, wg_ref[...]))

    halo = glu(xh_ref[...])
    a_ref[0:CONV_HALO, :] = jnp.where(i % tiles_per_seq == 0, 0.0, halo)
    a_ref[CONV_HALO:, :] = glu(xn_ref[...])

    rb = CONV_ROWS
    first = CONV_HALO - (CONV_KERNEL - 1)
    for r in range(tm // rb):
        cols = []
        for c in range(CONV_CHANNELS // LANES):
            cs = slice(c * LANES, (c + 1) * LANES)
            acc = dwb_ref[:, cs]
            for j in range(SUBLANES):
                rows = rb if j == 0 else rb + SUBLANES
                part = None
                for k in range(CONV_KERNEL):
                    if (first + k) % SUBLANES == j:
                        term = dww_ref[k:k + 1, cs] * a_ref[pl.ds(r * rb + first + k - j, rows), cs]
                        part = term if part is None else part + term
                if j > 0:
                    sh_ref[j] = part
                    part = sh_ref[j, pl.ds(j, rb), :]
                acc = acc + part
            cols.append(acc)
        y = jnp.concatenate(cols, axis=1)
        y = _layer_norm(y, lng_ref[...], lnb_ref[...])
        o_ref[r * rb:(r + 1) * rb, :] = (y * _sigmoid(y)).astype(BF16)


def _convmod(xn, w_in, layer, dw_w, dw_b, ln_g, ln_b, seq, tm=512):
    t, d = xn.shape
    c = CONV_CHANNELS
    col = CONV_COL // c
    ratio = tm // CONV_HALO
    const = lambda i: (0, 0)
    return pl.pallas_call(
        functools.partial(_convmod_kernel, tiles_per_seq=seq // tm),
        grid=(t // tm,),
        in_specs=[pl.BlockSpec((tm, d), lambda i: (i, 0)),
                  pl.BlockSpec((CONV_HALO, d), lambda i: (jnp.maximum(i * ratio - 1, 0), 0)),
                  pl.BlockSpec((None, d, c), lambda i: (layer, 0, col)),
                  pl.BlockSpec((None, d, c), lambda i: (layer, 0, col + 1)),
                  pl.BlockSpec((CONV_KERNEL, c), const), pl.BlockSpec((1, c), const),
                  pl.BlockSpec((1, c), const), pl.BlockSpec((1, c), const)],
        out_specs=pl.BlockSpec((tm, c), lambda i: (i, 0)),
        out_shape=jax.ShapeDtypeStruct((t, c), BF16),
        scratch_shapes=[pltpu.VMEM((tm + CONV_HALO, c), F32),
                        pltpu.VMEM((SUBLANES, CONV_ROWS + SUBLANES, LANES), F32)],
        compiler_params=_cparams("parallel"),
        name="conv_module",
    )(xn, xn, w_in, w_in, dw_w, dw_b, ln_g, ln_b)


def _sgu_kernel(xn_ref, wu_ref, wv_ref, lng_ref, lnb_ref, sw_ref, sbt_ref, o_ref):
    tm = xn_ref.shape[0]
    xn = xn_ref[...]
    u = _gelu_tanh(_dot(xn, wu_ref[...]))
    v = _layer_norm(_gelu_tanh(_dot(xn, wv_ref[...])), lng_ref[...], lnb_ref[...]).astype(BF16)
    row = lax.broadcasted_iota(jnp.int32, (SGU_CHUNK, SGU_CHUNK), 0)
    col = lax.broadcasted_iota(jnp.int32, (SGU_CHUNK, SGU_CHUNK), 1)
    for g in range(SGU_GROUPS):
        gs = slice(g * SGU_GROUP_DIM, (g + 1) * SGU_GROUP_DIM)
        w = jnp.where(col <= row, sw_ref[g], 0.0).astype(BF16)
        bias = sbt_ref[:, g:g + 1]
        for n in range(tm // SGU_CHUNK):
            rs = slice(n * SGU_CHUNK, (n + 1) * SGU_CHUNK)
            mixed = _dot(w, v[rs, gs]) + bias
            o_ref[rs, gs] = (u[rs, gs] * mixed).astype(BF16)


def _sgu(xn, w_in, layer, ln_g, ln_b, sgu_w, sgu_bt, tm=512):
    t, d = xn.shape
    c = SGU_WIDTH
    col = SGU_COL // c
    const = lambda i: (0, 0)
    return pl.pallas_call(
        _sgu_kernel,
        grid=(t // tm,),
        in_specs=[pl.BlockSpec((tm, d), lambda i: (i, 0)),
                  pl.BlockSpec((None, d, c), lambda i: (layer, 0, col)),
                  pl.BlockSpec((None, d, c), lambda i: (layer, 0, col + 1)),
                  pl.BlockSpec((1, c), const), pl.BlockSpec((1, c), const),
                  pl.BlockSpec((SGU_GROUPS, SGU_CHUNK, SGU_CHUNK), lambda i: (0, 0, 0)),
                  pl.BlockSpec((SGU_CHUNK, SGU_GROUPS), const)],
        out_specs=pl.BlockSpec((tm, c), lambda i: (i, 0)),
        out_shape=jax.ShapeDtypeStruct((t, c), BF16),
        compiler_params=_cparams("parallel"),
        name="sgu",
    )(xn, w_in, w_in, ln_g, ln_b, sgu_w, sgu_bt)


def _gates_kernel(xn_ref, w_ref, o_ref):
    o_ref[...] = _sigmoid(_dot(xn_ref[...], w_ref[...])).astype(BF16)


def _gates(xn, w_in, layer, tm=1024, tn=1024):
    t, d = xn.shape
    n = N_BRANCHES * D_MODEL
    col = GATE_COL // tn
    return pl.pallas_call(
        _gates_kernel,
        grid=(t // tm, n // tn),
        in_specs=[pl.BlockSpec((tm, d), lambda i, j: (i, 0)),
                  pl.BlockSpec((None, d, tn), lambda i, j: (layer, 0, col + j))],
        out_specs=pl.BlockSpec((tm, tn), lambda i, j: (i, j)),
        out_shape=jax.ShapeDtypeStruct((t, n), BF16),
        compiler_params=_cparams("parallel", "arbitrary"),
        name="gates",
    )(xn, w_in)


def _moba_kernel(q_ref, k_ref, vt_ref, km_ref, o_ref, qa_ref, s_ref, p_ref, *, qt):
    seq = q_ref.shape[1]
    n_tiles = seq // qt
    q_blocks = qt // MOBA_BLOCK
    nbp = km_ref.shape[1]
    nb_rows = SUBLANES * pl.cdiv(seq // MOBA_BLOCK, SUBLANES)

    def rows(i):
        return pl.ds(pl.multiple_of(i * qt, qt), qt)

    def gating_stage(i):
        q = q_ref[0, rows(i), :]
        gate = _dot_nt(km_ref[0, 0:nb_rows], q)
        blk_i = lax.broadcasted_iota(jnp.int32, gate.shape, 0)
        qry_i = lax.broadcasted_iota(jnp.int32, gate.shape, 1)
        own = i * q_blocks + sum((qry_i >= r * MOBA_BLOCK).astype(jnp.int32)
                                 for r in range(1, q_blocks))
        blkf = blk_i.astype(F32)
        past = blk_i < own
        g = jnp.where(past, gate, -jnp.inf)
        sel = jnp.zeros(gate.shape, jnp.bool_)
        for _ in range(MOBA_TOPK):
            top = jnp.max(g, axis=0, keepdims=True)
            first = jnp.min(jnp.where(g == top, blkf, F32(nbp)), axis=0, keepdims=True)
            pick = blkf == first
            sel = jnp.logical_or(sel, pick)
            g = jnp.where(pick, -jnp.inf, g)
        allowed = jnp.logical_or(jnp.logical_and(sel, past), blk_i == own)
        bias = jnp.where(allowed, 0.0, MASK_VALUE)
        if nbp > nb_rows:
            bias = jnp.concatenate([bias, jnp.full((nbp - nb_rows, qt), MASK_VALUE, F32)], axis=0)
        bias = bias.T.astype(BF16)
        qa_ref[i % 2, 0] = jnp.concatenate([q, bias], axis=1)
        qa_ref[i % 2, 1] = jnp.concatenate([q, jnp.full(bias.shape, MASK_VALUE, BF16)], axis=1)

    def own_group(i):
        return (i * q_blocks) // GROUP_BLOCKS

    def sweep_len(i):
        return own_group(i) + (1 - own_group(i) % 2) + 1

    def qk_stage(i, t, slot):
        own_g = own_group(i)
        st = pl.multiple_of(jnp.minimum(t, own_g) * KEY_GROUP, KEY_GROUP)
        masked_pass = jnp.logical_and(t == own_g, own_g % 2 == 0)
        s = _dot_nt(k_ref[0, pl.ds(st, KEY_GROUP), :], qa_ref[i % 2, masked_pass.astype(jnp.int32)])
        s_ref[slot] = s
        return jnp.max(s, axis=0, keepdims=True)

    def softmax_stage(slot, m, s_max):
        m_new = jnp.maximum(m, s_max)
        p_ref[slot] = jnp.exp2(s_ref[slot] - m_new).astype(BF16)
        return m_new, jnp.exp2(m - m_new)

    def causal_softmax_stage(slot, m):
        blk = MOBA_BLOCK
        lower = (lax.broadcasted_iota(jnp.int32, (blk, blk), 0)
                 <= lax.broadcasted_iota(jnp.int32, (blk, blk), 1))
        for r in range(q_blocks):
            d = slice(r * blk, (r + 1) * blk)
            s_ref[slot, d, d] = jnp.where(lower, s_ref[slot, d, d], MASK_VALUE)
        s = s_ref[slot]
        m_new = jnp.maximum(m, jnp.max(s, axis=0, keepdims=True))
        p_ref[slot] = jnp.exp2(s - m_new).astype(BF16)
        return jnp.exp2(m - m_new)

    def pv_stage(i, t, slot, acc, alpha):
        return alpha * acc + _dot(vt_ref[0, jnp.minimum(t, own_group(i))], p_ref[slot])

    def fill(i):
        max0 = qk_stage(i, 0, 0)
        max1 = qk_stage(i, 1, 1)
        m, a0 = softmax_stage(0, jnp.full((1, qt), MASK_VALUE, F32), max0)
        return m, a0, max1

    def tile(i, carry):
        m, a0, max1 = carry

        def pair(u, carry):
            m, acc, a0, max1 = carry
            t = 2 + 2 * u
            max0 = qk_stage(i, t, 0)
            m, a1 = softmax_stage(1, m, max1)
            acc = pv_stage(i, t - 2, 0, acc, a0)
            max1 = qk_stage(i, t + 1, 1)
            m, a0 = softmax_stage(0, m, max0)
            acc = pv_stage(i, t - 1, 1, acc, a1)
            return m, acc, a0, max1

        n_all = sweep_len(i)
        acc = jnp.zeros((VT_ROWS, qt), F32)
        m, acc, a0, _ = lax.fori_loop(0, (n_all - 2) // 2, pair, (m, acc, a0, max1))
        nxt = jnp.minimum(i + 1, n_tiles - 1)
        max0 = qk_stage(nxt, 0, 0)
        a1 = causal_softmax_stage(1, m)
        acc = pv_stage(i, n_all - 2, 0, acc, a0)
        max1 = qk_stage(nxt, 1, 1)
        gating_stage(jnp.minimum(i + 2, n_tiles - 1))
        acc = pv_stage(i, n_all - 1, 1, acc, a1)
        m, a0 = softmax_stage(0, jnp.full((1, qt), MASK_VALUE, F32), max0)
        out_t = acc[0:HEAD_DIM] / acc[HEAD_DIM:HEAD_DIM + 1]
        o_ref[0, rows(i), :] = out_t.T.astype(BF16)
        return m, a0, max1

    gating_stage(0)
    if n_tiles > 1:
        gating_stage(1)
    lax.fori_loop(0, n_tiles, tile, fill(0))


def _moba(q, k_aug, vt, kmean, qt=1024):
    b, s, _ = q.shape
    nbp = kmean.shape[1]
    assert qt == KEY_GROUP, "a tile's own blocks are the diagonal of one key group"
    once = pl.Buffered(1)
    return pl.pallas_call(
        functools.partial(_moba_kernel, qt=qt),
        grid=(b, N_HEADS),
        in_specs=[pl.BlockSpec((1, s, HEAD_DIM), lambda bi, h: (bi, 0, h)),
                  pl.BlockSpec((1, s, 2 * HEAD_DIM), lambda bi, h: (bi, 0, h), pipeline_mode=once),
                  pl.BlockSpec((1, s // KEY_GROUP, VT_ROWS, KEY_GROUP), lambda bi, h: (h, bi, 0, 0),
                               pipeline_mode=once),
                  pl.BlockSpec((1, nbp, HEAD_DIM), lambda bi, h: (bi, 0, h))],
        out_specs=pl.BlockSpec((1, s, HEAD_DIM), lambda bi, h: (bi, 0, h)),
        out_shape=jax.ShapeDtypeStruct((b, s, ATTN_WIDTH), BF16),
        scratch_shapes=[pltpu.VMEM((2, 2, qt, HEAD_DIM + nbp), BF16),
                        pltpu.VMEM((2, KEY_GROUP, qt), F32),
                        pltpu.VMEM((2, KEY_GROUP, qt), BF16)],
        compiler_params=_cparams("parallel", "arbitrary"),
        name="moba_attention",
    )(q, k_aug, vt, kmean)


def _merge_kernel(x_ref, a_ref, sb_ref, at_ref, gt_ref, wca_ref, wsg_ref, wat_ref, wo_ref,
                  gpost_ref, o_ref):
    d = D_MODEL
    merged = (gt_ref[:, 0:d].astype(F32) * _dot(a_ref[...], wca_ref[...])
              + gt_ref[:, d:2 * d].astype(F32) * _dot(sb_ref[...], wsg_ref[...])
              + gt_ref[:, 2 * d:3 * d].astype(F32) * _dot(at_ref[...], wat_ref[...]))
    h = _dot(merged.astype(BF16), wo_ref[...])
    o_ref[...] = x_ref[...] + _rms(h, gpost_ref[...])


def _merge(x, a_act, sb, attn, gates, conv_out, sgu_out, attn_out, w_o, layer, g_post, tm=256):
    t, d = x.shape
    const = lambda i: (0, 0)
    wspec = lambda k: pl.BlockSpec((None, k, d), lambda i: (layer, 0, 0), pipeline_mode=pl.Buffered(1))
    return pl.pallas_call(
        _merge_kernel,
        grid=(t // tm,),
        in_specs=[pl.BlockSpec((tm, d), lambda i: (i, 0)),
                  pl.BlockSpec((tm, CONV_CHANNELS), lambda i: (i, 0)),
                  pl.BlockSpec((tm, SGU_WIDTH), lambda i: (i, 0)),
                  pl.BlockSpec((tm, ATTN_WIDTH), lambda i: (i, 0)),
                  pl.BlockSpec((tm, 3 * d), lambda i: (i, 0)),
                  wspec(CONV_CHANNELS), wspec(SGU_WIDTH), wspec(ATTN_WIDTH), wspec(d),
                  pl.BlockSpec((1, d), const)],
        out_specs=pl.BlockSpec((tm, d), lambda i: (i, 0)),
        out_shape=jax.ShapeDtypeStruct((t, d), F32),
        compiler_params=_cparams("parallel"),
        name="merge",
    )(x, a_act, sb, attn, gates, conv_out, sgu_out, attn_out, w_o, g_post)


def _ffn_kernel(x_ref, xh_ref, gpre_ref, wa_ref, wb_ref, dww_ref, dwb_ref, wo_ref, gpost_ref,
                o_ref, hn_ref, acc_ref, *, tiles_per_seq):
    i = pl.program_id(0)
    j = pl.program_id(1)
    tm = x_ref.shape[0]

    @pl.when(j == 0)
    def _():
        halo = _rms(xh_ref[...], gpre_ref[...])
        hn_ref[0:FFN_HALO, :] = jnp.where(i % tiles_per_seq == 0, 0.0, halo).astype(BF16)
        hn_ref[FFN_HALO:, :] = _rms(x_ref[...], gpre_ref[...]).astype(BF16)
        acc_ref[...] = jnp.zeros_like(acc_ref)

    a = _dot(hn_ref[...], wa_ref[...])
    b = _dot(hn_ref[FFN_HALO:, :], wb_ref[...])
    conv = dwb_ref[...] + dww_ref[2:3, :] * a[FFN_HALO:]
    for k in (1, 2):
        conv = conv + dww_ref[2 - k:3 - k, :] * pltpu.roll(a, k, 0)[FFN_HALO:]
    act = (_gelu_tanh(conv) * b).astype(BF16)
    acc_ref[...] += _dot(act, wo_ref[...])

    @pl.when(j == pl.num_programs(1) - 1)
    def _():
        o_ref[...] = x_ref[...] + _rms(acc_ref[...], gpost_ref[...])


def _ffn(x, g_pre, ffn_in, dw_w, dw_b, ffn_out, layer, g_post, seq, tm=512, tf=512):
    t, d = x.shape
    nf = D_FF // tf
    ratio = tm // FFN_HALO
    return pl.pallas_call(
        functools.partial(_ffn_kernel, tiles_per_seq=seq // tm),
        grid=(t // tm, nf),
        in_specs=[pl.BlockSpec((tm, d), lambda i, j: (i, 0)),
                  pl.BlockSpec((FFN_HALO, d), lambda i, j: (jnp.maximum(i * ratio - 1, 0), 0)),
                  pl.BlockSpec((1, d), lambda i, j: (0, 0)),
                  pl.BlockSpec((None, d, tf), lambda i, j: (layer, 0, j)),
                  pl.BlockSpec((None, d, tf), lambda i, j: (layer, 0, j + nf)),
                  pl.BlockSpec((FFN_CONV_KERNEL, tf), lambda i, j: (0, j)),
                  pl.BlockSpec((1, tf), lambda i, j: (0, j)),
                  pl.BlockSpec((None, tf, d), lambda i, j: (layer, j, 0)),
                  pl.BlockSpec((1, d), lambda i, j: (0, 0))],
        out_specs=pl.BlockSpec((tm, d), lambda i, j: (i, 0)),
        out_shape=jax.ShapeDtypeStruct((t, d), F32),
        scratch_shapes=[pltpu.VMEM((tm + FFN_HALO, d), BF16), pltpu.VMEM((tm, d), F32)],
        compiler_params=_cparams("parallel", "arbitrary"),
        name="conv_ffn",
    )(x, x, g_pre, ffn_in, ffn_in, dw_w, dw_b, ffn_out, g_post)


def _ple_kernel(x_ref, p_ref, g_ref, wg_ref, wp_ref, gn_ref, o_ref, *xn_ref):
    x = x_ref[...]
    gate = _sigmoid(_dot(_rms(x, g_ref[...]).astype(BF16), wg_ref[...]))
    y = x + gate * _dot(p_ref[...].astype(BF16), wp_ref[...])
    o_ref[...] = y
    if xn_ref:
        xn_ref[0][...] = _rms(y, gn_ref[...]).astype(BF16)


def _ple(x, p, g, w_gate, w_proj, layer, g_next, emit_next, tm=512):
    t, d = x.shape
    const = lambda i: (0, 0)
    tile = pl.BlockSpec((tm, d), lambda i: (i, 0))
    out = pl.pallas_call(
        _ple_kernel,
        grid=(t // tm,),
        in_specs=[tile,
                  pl.BlockSpec((None, tm, PLE_DIM), lambda i: (layer, i, 0)),
                  pl.BlockSpec((1, d), const),
                  pl.BlockSpec((None, d, d), lambda i: (layer, 0, 0), pipeline_mode=pl.Buffered(1)),
                  pl.BlockSpec((None, PLE_DIM, d), lambda i: (layer, 0, 0), pipeline_mode=pl.Buffered(1)),
                  pl.BlockSpec((1, d), const)],
        out_specs=[tile, tile] if emit_next else [tile],
        out_shape=([jax.ShapeDtypeStruct((t, d), F32), jax.ShapeDtypeStruct((t, d), BF16)]
                   if emit_next else [jax.ShapeDtypeStruct((t, d), F32)]),
        compiler_params=_cparams("parallel"),
        name="ple",
    )(x, p, g, w_gate, w_proj, g_next)
    return out if emit_next else (out[0], None)


def _rope_tables(seq):
    half = ROPE_DIM // 2
    inv = F32(ROPE_THETA) ** (-jnp.arange(0, ROPE_DIM, 2, dtype=F32) / ROPE_DIM)
    ang = jnp.arange(seq, dtype=jnp.int32).astype(F32)[:, None] * inv[None, :]
    cos, sin = jnp.cos(ang), jnp.sin(ang)
    zeros = jnp.zeros((seq, LANES - ROPE_DIM), F32)
    z16 = jnp.zeros((seq, half), F32)
    cos_t = jnp.concatenate([cos, cos, jnp.ones((seq, LANES - ROPE_DIM), F32)], axis=1)
    sin_lo = jnp.concatenate([-sin, z16, zeros], axis=1)
    sin_hi = jnp.concatenate([z16, sin, zeros], axis=1)
    return cos_t, sin_lo, sin_hi


def kernel(x, p, mix_norm_pre, mix_norm_post, w_in, conv_dw_w, conv_dw_b, conv_norm_g, conv_norm_b, conv_out, sgu_norm_g, sgu_norm_b, sgu_w, sgu_b, sgu_out, attn_out, w_o, ffn_norm_pre, ffn_norm_post, ffn_in, ffn_dw_w, ffn_dw_b, ffn_out, ple_norm, ple_gate, ple_proj):
    bsz, seq, d = x.shape
    depth = w_in.shape[0]
    assert d == D_MODEL and seq % KEY_GROUP == 0
    t = bsz * seq
    nb = seq // MOBA_BLOCK
    nbp = LANES * pl.cdiv(nb, LANES)
    assert nbp == LANES, "block-mask columns share one 256-deep contraction with the head dim"

    cos_t, sin_lo, sin_hi = _rope_tables(seq)
    row = lambda v: v.reshape(1, -1)

    w_in, conv_out, sgu_out, attn_out, w_o, ffn_in, ffn_out, ple_gate, ple_proj = (
        a.astype(BF16) for a in (w_in, conv_out, sgu_out, attn_out, w_o, ffn_in, ffn_out,
                                 ple_gate, ple_proj))
    p = p.reshape(depth, t, PLE_DIM)
    xf = x.reshape(t, d)
    xn = _prenorm(xf, row(mix_norm_pre[0]))
    for i in range(depth):
        q, k_aug, vt, kmean = _qkv(xn, w_in, i, cos_t, sin_lo, sin_hi, seq)
        a_act = _convmod(xn, w_in, i, conv_dw_w[i], row(conv_dw_b[i]),
                         row(conv_norm_g[i]), row(conv_norm_b[i]), seq)
        sb = _sgu(xn, w_in, i, row(sgu_norm_g[i]), row(sgu_norm_b[i]), sgu_w[i], sgu_b[i].T)
        gates = _gates(xn, w_in, i)
        kmean = kmean.reshape(bsz, nb, ATTN_WIDTH)
        kmean = jnp.pad(kmean, ((0, 0), (0, nbp - nb), (0, 0))).astype(BF16)
        attn = _moba(q.reshape(bsz, seq, ATTN_WIDTH), k_aug.reshape(bsz, seq, 2 * ATTN_WIDTH), vt,
                     kmean).reshape(t, ATTN_WIDTH)
        xf = _merge(xf, a_act, sb, attn, gates, conv_out, sgu_out, attn_out, w_o, i,
                    row(mix_norm_post[i]))
        xf = _ffn(xf, row(ffn_norm_pre[i]), ffn_in, ffn_dw_w[i], row(ffn_dw_b[i]), ffn_out, i,
                  row(ffn_norm_post[i]), seq)
        last = i == depth - 1
        xf, xn = _ple(xf, p, row(ple_norm[i]), ple_gate, ple_proj, i,
                      row(mix_norm_pre[0 if last else i + 1]), not last)
    return xf.reshape(bsz, seq, d)
```
